```python
import jax, jax.numpy as jnp
from jax import lax
import numpy as np

D_MODEL = 1024
BATCH = 4
SEQ = 8192
DEPTH = 4

GRID_W = 64
CTX_LEN = 256
N_MIXERS = 2
D_RNN = 1024
RG_HEADS = 8
RG_BLOCK = D_RNN // RG_HEADS
RG_CONV = 4
RG_C = 8.0
POOL_WINDOWS = (2, 4, 8, 16)
N_POOL_GROUPS = len(POOL_WINDOWS)
POOL_GROUP = D_MODEL // N_POOL_GROUPS
D_FF = 3 * D_MODEL
FFN_CONV = 3
N_MOD = 6
N_RG_LAYERS = (DEPTH + 1) // 2
N_POOL_LAYERS = DEPTH // 2
EPS = 1e-6

kernel_name = "hybrid_rglru_pool_dit_trunk"


def _rms_norm(x, g):
    xf = x.astype(jnp.float32)
    y = xf * lax.rsqrt(jnp.mean(xf * xf, axis=-1, keepdims=True) + EPS)
    return (y * g.astype(jnp.float32)).astype(x.dtype)


def _modulate(h, shift, scale):
    return h * (1.0 + scale) + shift


def _dwconv(x, w, b, pad):
    c = x.shape[-1]
    y = lax.conv_general_dilated(x, w.astype(x.dtype)[:, None, :], window_strides=(1,), padding=(pad,),
                                 dimension_numbers=("NWC", "WIO", "NWC"), feature_group_count=c)
    return y + b.astype(x.dtype)


def _linear_scan(a, b, h0, reverse):
    if reverse:
        b = b.at[:, -1].add(a[:, -1] * h0)
    else:
        b = b.at[:, 0].add(a[:, 0] * h0)

    def comb(l, r):
        return l[0] * r[0], r[0] * l[1] + r[1]

    _, h = lax.associative_scan(comb, (a, b), reverse=reverse, axis=1)
    return h


def _rg_lru_terms(u, wa, ba, wx, bx, lam):
    bsz, n, _ = u.shape
    uh = u.reshape(bsz, n, RG_HEADS, RG_BLOCK)
    r = jax.nn.sigmoid((jnp.einsum("bthi,hij->bthj", uh, wa.astype(u.dtype)).reshape(bsz, n, D_RNN)
                        + ba.astype(u.dtype)).astype(jnp.float32))
    i = jax.nn.sigmoid((jnp.einsum("bthi,hij->bthj", uh, wx.astype(u.dtype)).reshape(bsz, n, D_RNN)
                        + bx.astype(u.dtype)).astype(jnp.float32))
    log_a = -RG_C * r * jax.nn.softplus(-lam.astype(jnp.float32))
    mult = jnp.sqrt(-jnp.expm1(2.0 * log_a))
    return jnp.exp(log_a), mult * (i * u.astype(jnp.float32))


def _rglru_mixer(h_ctx, h_lat, w_in, conv_w, conv_b, wa, ba, wx, bx, lam, w_out, need_ctx_out):
    def branches(h):
        z = h @ w_in.astype(h.dtype)
        g, rec = z[..., :D_RNN], z[..., D_RNN:]
        u = _dwconv(rec, conv_w, conv_b, (RG_CONV // 2, RG_CONV - 1 - RG_CONV // 2))
        return jax.nn.gelu(g), u

    g_ctx, u_ctx = branches(h_ctx)
    g_lat, u_lat = branches(h_lat)
    bsz = h_lat.shape[0]
    y_ctx = jnp.zeros(u_ctx.shape, jnp.float32)
    y_lat = jnp.zeros(u_lat.shape, jnp.float32)
    for d, rev in enumerate((False, True)):
        a_c, b_c = _rg_lru_terms(u_ctx, wa[d], ba[d], wx[d], bx[d], lam[d])
        hc = _linear_scan(a_c, b_c, jnp.zeros((bsz, D_RNN), jnp.float32), rev)
        h_final = hc[:, 0] if rev else hc[:, -1]
        a_l, b_l = _rg_lru_terms(u_lat, wa[d], ba[d], wx[d], bx[d], lam[d])
        y_lat = y_lat + _linear_scan(a_l, b_l, h_final, rev)
        if need_ctx_out:
            y_ctx = y_ctx + hc
    out_lat = (g_lat * y_lat.astype(h_lat.dtype)) @ w_out.astype(h_lat.dtype)
    out_ctx = (g_ctx * y_ctx.astype(h_ctx.dtype)) @ w_out.astype(h_ctx.dtype) if need_ctx_out else None
    return out_ctx, out_lat


def _window_sum(x, w, axis):
    n = x.shape[axis]
    cs = jnp.cumsum(x, axis=axis)
    s = jnp.concatenate([jnp.zeros_like(lax.slice_in_dim(cs, 0, 1, axis=axis)), cs], axis=axis)
    t = jnp.arange(n)
    lo = jnp.clip(t - w // 2, 0, n)
    hi = jnp.clip(t + w // 2, 0, n)
    win = jnp.take(s, hi, axis=axis) - jnp.take(s, lo, axis=axis)
    cnt = (hi - lo).astype(jnp.float32).reshape((n,) + (1,) * (x.ndim - axis - 1))
    return win, cnt


def _pool_mixer(h, w_grp, b, scale, rows):
    hf = h.astype(jnp.float32)
    bsz, n, _ = h.shape
    outs = []
    for g, w in enumerate(POOL_WINDOWS):
        hg = hf[..., g * POOL_GROUP:(g + 1) * POOL_GROUP]
        if rows is None:
            s, cnt = _window_sum(hg, w, 1)
            pooled = s / cnt
        else:
            hg2 = hg.reshape(bsz, rows, GRID_W, POOL_GROUP)
            s, cr = _window_sum(hg2, w, 1)
            s, cc = _window_sum(s, w, 2)
            pooled = (s / (cr * cc)).reshape(bsz, n, POOL_GROUP)
        outs.append(jnp.einsum("btc,cd->btd", (pooled - hg).astype(h.dtype), w_grp[g].astype(h.dtype)))
    y = jnp.concatenate(outs, axis=-1) + b.astype(h.dtype)
    return y * scale.astype(h.dtype)


def _conv_ffn(h, w_up, conv_w, conv_b, w_down):
    z = h @ w_up.astype(h.dtype)
    u, v = z[..., :D_FF], z[..., D_FF:]
    u = _dwconv(u, conv_w, conv_b, (FFN_CONV // 2, FFN_CONV // 2))
    return (jax.nn.gelu(u) * v) @ w_down.astype(h.dtype)


def setup_inputs(seed: int = 0) -> dict:
    key = jax.random.key(seed)
    ks = iter(jax.random.split(key, 32))
    f32 = jnp.float32

    def nrm(shape, scale):
        return jax.random.normal(next(ks), shape, f32) * scale

    a0 = jax.random.uniform(next(ks), (N_RG_LAYERS, 2, D_RNN), f32, 0.9, 0.999)
    return {
        "x": nrm((BATCH, SEQ, D_MODEL), 1.0),
        "c": nrm((BATCH, D_MODEL), 1.0),
        "ctx": nrm((BATCH, CTX_LEN, D_MODEL), 1.0),
        "c_ctx": nrm((D_MODEL,), 1.0),
        "ada_w": nrm((DEPTH, D_MODEL, N_MOD * D_MODEL), 0.5 * D_MODEL ** -0.5),
        "ada_b": nrm((DEPTH, N_MOD * D_MODEL), 0.01),
        "norm_pre_mix": 1.0 + nrm((DEPTH, D_MODEL), 0.05),
        "norm_post_mix": 1.0 + nrm((DEPTH, D_MODEL), 0.05),
        "norm_pre_ffn": 1.0 + nrm((DEPTH, D_MODEL), 0.05),
        "norm_post_ffn": 1.0 + nrm((DEPTH, D_MODEL), 0.05),
        "rg_w_in": nrm((N_RG_LAYERS, D_MODEL, 2 * D_RNN), D_MODEL ** -0.5),
        "rg_conv_w": nrm((N_RG_LAYERS, RG_CONV, D_RNN), RG_CONV ** -0.5),
        "rg_conv_b": nrm((N_RG_LAYERS, D_RNN), 0.01),
        "rg_gate_a_w": nrm((N_RG_LAYERS, 2, RG_HEADS, RG_BLOCK, RG_BLOCK), RG_BLOCK ** -0.5),
        "rg_gate_a_b": nrm((N_RG_LAYERS, 2, D_RNN), 0.01),
        "rg_gate_x_w": nrm((N_RG_LAYERS, 2, RG_HEADS, RG_BLOCK, RG_BLOCK), RG_BLOCK ** -0.5),
        "rg_gate_x_b": nrm((N_RG_LAYERS, 2, D_RNN), 0.01),
        "rg_lambda": jnp.log(a0) - jnp.log1p(-a0),
        "rg_w_out": nrm((N_RG_LAYERS, D_RNN, D_MODEL), D_RNN ** -0.5),
        "pool_w": nrm((N_POOL_LAYERS, N_POOL_GROUPS, POOL_GROUP, POOL_GROUP), POOL_GROUP ** -0.5),
        "pool_b": nrm((N_POOL_LAYERS, D_MODEL), 0.01),
        "pool_scale": 1.0 + nrm((N_POOL_LAYERS, D_MODEL), 0.1),
        "ffn_w_up": nrm((DEPTH, D_MODEL, 2 * D_FF), D_MODEL ** -0.5),
        "ffn_conv_w": nrm((DEPTH, FFN_CONV, D_FF), FFN_CONV ** -0.5),
        "ffn_conv_b": nrm((DEPTH, D_FF), 0.01),
        "ffn_w_down": nrm((DEPTH, D_FF, D_MODEL), D_FF ** -0.5),
    }


def reference(x, c, ctx, c_ctx, ada_w, ada_b, norm_pre_mix, norm_post_mix, norm_pre_ffn, norm_post_ffn,
              rg_w_in, rg_conv_w, rg_conv_b, rg_gate_a_w, rg_gate_a_b, rg_gate_x_w, rg_gate_x_b,
              rg_lambda, rg_w_out, pool_w, pool_b, pool_scale, ffn_w_up, ffn_conv_w, ffn_conv_b,
              ffn_w_down):
    rows = x.shape[1] // GRID_W
    sc = jax.nn.silu(c)
    sc_ctx = jax.nn.silu(c_ctx)
    for i in range(DEPTH):
        last = i == DEPTH - 1
        mod_l = jnp.split((sc @ ada_w[i] + ada_b[i])[:, None, :], N_MOD, axis=-1)
        mod_c = jnp.split(sc_ctx @ ada_w[i] + ada_b[i], N_MOD, axis=-1)
        sh_m, sc_m, g_m, sh_f, sc_f, g_f = mod_l
        csh_m, csc_m, cg_m, csh_f, csc_f, cg_f = mod_c

        h_lat = _modulate(_rms_norm(x, norm_pre_mix[i]), sh_m, sc_m)
        h_ctx = _modulate(_rms_norm(ctx, norm_pre_mix[i]), csh_m, csc_m)
        j = i // N_MIXERS
        if i % N_MIXERS == 0:
            y_ctx, y_lat = _rglru_mixer(h_ctx, h_lat, rg_w_in[j], rg_conv_w[j], rg_conv_b[j],
                                        rg_gate_a_w[j], rg_gate_a_b[j], rg_gate_x_w[j], rg_gate_x_b[j],
                                        rg_lambda[j], rg_w_out[j], not last)
        else:
            y_lat = _pool_mixer(h_lat, pool_w[j], pool_b[j], pool_scale[j], rows)
            y_ctx = None if last else _pool_mixer(h_ctx, pool_w[j], pool_b[j], pool_scale[j], None)
        x = x + g_m * _rms_norm(y_lat, norm_post_mix[i])
        if not last:
            ctx = ctx + cg_m * _rms_norm(y_ctx, norm_post_mix[i])

        f_lat = _conv_ffn(_modulate(_rms_norm(x, norm_pre_ffn[i]), sh_f, sc_f),
                          ffn_w_up[i], ffn_conv_w[i], ffn_conv_b[i], ffn_w_down[i])
        x = x + g_f * _rms_norm(f_lat, norm_post_ffn[i])
        if not last:
            f_ctx = _conv_ffn(_modulate(_rms_norm(ctx, norm_pre_ffn[i]), csh_f, csc_f),
                              ffn_w_up[i], ffn_conv_w[i], ffn_conv_b[i], ffn_w_down[i])
            ctx = ctx + cg_f * _rms_norm(f_ctx, norm_post_ffn[i])
    return x
```

```python
import functools
import math

import numpy as np
import jax
import jax.numpy as jnp
from jax import lax
from jax.experimental import pallas as pl
from jax.experimental.pallas import tpu as pltpu

D_MODEL = 1024
GRID_W = 64
N_MOD = 6
RG_HEADS = 8
RG_BLOCK = D_MODEL // RG_HEADS
RG_CONV = 4
RG_C = 8.0
POOL_WINDOWS = (2, 4, 8, 16)
POOL_GROUP = D_MODEL // len(POOL_WINDOWS)
D_FF = 3 * D_MODEL
EPS = 1e-6

SUBLANES = 8
HALO = 8
POOL_CHUNK = 256
VMEM_LIMIT = 56 * 1024 * 1024

F32 = jnp.float32
BF16 = jnp.bfloat16


def _rms(xf, g):
    ms = jnp.mean(xf * xf, axis=-1, keepdims=True)
    return xf * lax.rsqrt(ms + EPS) * g


def _gelu(x):
    c = math.sqrt(2.0 / math.pi)
    return x * (0.5 * (1.0 + jnp.tanh(c * (x + 0.044715 * (x * x * x)))))


def _sigmoid(x):
    return 1.0 / (1.0 + jnp.exp(-x))


def _const_spec(shape):
    nd = len(shape)
    return pl.BlockSpec(shape, lambda *_: (0,) * nd, pipeline_mode=pl.Buffered(1))


def _tile_specs(tm, nt, reverse):
    per = tm // HALO
    last = nt * per - 1

    def tile(j):
        return (nt - 1 - j) if reverse else j

    main = pl.BlockSpec((1, tm, D_MODEL), lambda b, j: (b, tile(j), 0))
    prev = pl.BlockSpec((1, HALO, D_MODEL), lambda b, j: (b, jnp.maximum(tile(j) * per - 1, 0), 0))
    nxt = pl.BlockSpec((1, HALO, D_MODEL), lambda b, j: (b, jnp.minimum((tile(j) + 1) * per, last), 0))
    return main, prev, nxt


def _mod_spec(n_mod_rows):
    if n_mod_rows == 1:
        return pl.BlockSpec((1, N_MOD, D_MODEL), lambda b, j: (0, 0, 0))
    return pl.BlockSpec((1, N_MOD, D_MODEL), lambda b, j: (b, 0, 0))


def _ada_kernel(ct_ref, w_ref, b_ref, o_ref, *, n_rows):
    ct = ct_ref[...]
    st = ct * _sigmoid(ct)
    w = w_ref[0]
    rows = [jnp.sum(w * st[:, m:m + 1], axis=0, keepdims=True) for m in range(n_rows)]
    rows += [jnp.zeros_like(rows[0])] * (SUBLANES - n_rows)
    o_ref[0] = jnp.concatenate(rows, axis=0) + b_ref[0]


def _ada(ct, ada_w, ada_b, n_rows):
    depth, d, n = ada_w.shape
    tn = 1536
    return pl.pallas_call(
        functools.partial(_ada_kernel, n_rows=n_rows),
        grid=(depth, n // tn),
        in_specs=[pl.BlockSpec((d, SUBLANES), lambda i, k: (0, 0)),
                  pl.BlockSpec((1, d, tn), lambda i, k: (i, 0, k)),
                  pl.BlockSpec((1, 1, tn), lambda i, k: (i, 0, k))],
        out_specs=pl.BlockSpec((1, SUBLANES, tn), lambda i, k: (i, 0, k)),
        out_shape=jax.ShapeDtypeStruct((depth, SUBLANES, n), F32),
        compiler_params=pltpu.CompilerParams(dimension_semantics=("parallel", "parallel"),
                                             vmem_limit_bytes=VMEM_LIMIT),
        name="ada_mod",
    )(ct, ada_w, ada_b.reshape(depth, 1, n))


def _prep_ext(x_ref, xp_ref, xn_ref, g_pre, shift, scale, tile, nt):
    def prep(v):
        return _rms(v, g_pre) * (1.0 + scale) + shift

    x = x_ref[0]
    hp = jnp.where(tile > 0, prep(xp_ref[0]), 0.0)
    hn = jnp.where(tile < nt - 1, prep(xn_ref[0]), 0.0)
    return x, jnp.concatenate([hp, prep(x), hn], axis=0)


def _shift_rows(z, k, tm):
    if k == 0:
        return z[HALO:HALO + tm]
    return pltpu.roll(z, (-k) % z.shape[0], axis=0)[HALO:HALO + tm]


def _ffn_kernel(x_ref, xp_ref, xn_ref, mod_ref, gpre_ref, gpost_ref, wup_ref, cw_ref, cb_ref, wdn_ref,
                o_ref, hext_ref, acc_ref, *, tm, nt, fc):
    j = pl.program_id(1)
    shift, scale, gate = mod_ref[0, 3:4, :], mod_ref[0, 4:5, :], mod_ref[0, 5:6, :]
    x, hext = _prep_ext(x_ref, xp_ref, xn_ref, gpre_ref[...], shift, scale, j, nt)
    hext_ref[...] = hext.astype(BF16)
    for c in range(D_FF // fc):
        lo = c * fc
        hb = hext_ref[...]
        zu = jnp.dot(hb, wup_ref[:, lo:lo + fc], preferred_element_type=F32)
        zv = jnp.dot(hb, wup_ref[:, D_FF + lo:D_FF + lo + fc], preferred_element_type=F32)
        cw = cw_ref[:, lo:lo + fc]
        u = (cw[0:1] * _shift_rows(zu, -1, tm) + cw[1:2] * _shift_rows(zu, 0, tm)
             + cw[2:3] * _shift_rows(zu, 1, tm) + cb_ref[:, lo:lo + fc])
        act = (_gelu(u) * zv[HALO:HALO + tm]).astype(BF16)
        part = jnp.dot(act, wdn_ref[lo:lo + fc, :], preferred_element_type=F32)
        if c == 0:
            acc_ref[...] = part
        else:
            acc_ref[...] += part
    o_ref[0] = x + gate * _rms(acc_ref[...], gpost_ref[...])


def _ffn(x, mod, g_pre, g_post, w_up, conv_w, conv_b, w_down, *, tm, fc=512):
    b, t, d = x.shape
    nt = t // tm
    main, prev, nxt = _tile_specs(tm, nt, False)
    kern = functools.partial(_ffn_kernel, tm=tm, nt=nt, fc=fc)
    return pl.pallas_call(
        kern,
        grid=(b, nt),
        in_specs=[main, prev, nxt, _mod_spec(mod.shape[0]),
                  _const_spec((1, d)), _const_spec((1, d)),
                  _const_spec(w_up.shape), _const_spec(conv_w.shape), _const_spec((1, D_FF)),
                  _const_spec(w_down.shape)],
        out_specs=pl.BlockSpec((1, tm, d), lambda bb, j: (bb, j, 0)),
        out_shape=jax.ShapeDtypeStruct(x.shape, F32),
        scratch_shapes=[pltpu.VMEM((tm + 2 * HALO, d), BF16), pltpu.VMEM((tm, d), F32)],
        compiler_params=pltpu.CompilerParams(dimension_semantics=("parallel", "parallel"),
                                             vmem_limit_bytes=VMEM_LIMIT),
        name="conv_ffn",
    )(x, x, x, mod, g_pre, g_post, w_up, conv_w, conv_b, w_down)


def _seg_pitch(seg):
    p = seg + SUBLANES
    assert (p // SUBLANES) % 2 == 1, seg
    return p


def _scan_tile(a_scr, b_scr, carry_ref, *, seg, reverse):
    pitch = _seg_pitch(seg)
    nh = a_scr.shape[0]
    sub = lax.broadcasted_iota(jnp.int32, (SUBLANES, RG_BLOCK), 0)

    def rows(jr):
        return pl.ds(jr, SUBLANES, stride=pitch)

    def pass1(i, st):
        jr = (seg - 1 - i) if reverse else i
        out = []
        for hd in range(nh):
            h, p = st[2 * hd], st[2 * hd + 1]
            av = a_scr[hd, rows(jr), :]
            h = av * h + b_scr[hd, rows(jr), :]
            p = p * av
            b_scr[hd, rows(jr), :] = h
            a_scr[hd, rows(jr), :] = p
            out += [h, p]
        return tuple(out)

    init = tuple(jnp.zeros((SUBLANES, RG_BLOCK), F32) if k % 2 == 0 else jnp.ones((SUBLANES, RG_BLOCK), F32)
                 for k in range(2 * nh))
    st = lax.fori_loop(0, seg, pass1, init)

    first = SUBLANES - 1 if reverse else 0
    last = 0 if reverse else SUBLANES - 1
    cins = []
    for hd in range(nh):
        e, pe = st[2 * hd], st[2 * hd + 1]
        carry = jnp.broadcast_to(carry_ref[hd], (SUBLANES, RG_BLOCK))
        cin = carry
        for _ in range(SUBLANES - 1):
            nxt = pltpu.roll(e + pe * cin, (SUBLANES - 1) if reverse else 1, axis=0)
            cin = jnp.where(sub == first, carry, nxt)
        cins.append(cin)
        end = e + pe * cin
        carry_ref[hd] = end[last:last + 1, :]

    def pass2(jr, _):
        for hd in range(nh):
            b_scr[hd, rows(jr), :] = b_scr[hd, rows(jr), :] + a_scr[hd, rows(jr), :] * cins[hd]
        return 0

    lax.fori_loop(0, seg, pass2, 0)


def _rg_kernel(*refs, tm, nt, reverse, final):
    if final:
        (x_ref, xp_ref, xn_ref, mod_ref, gpre_ref, wr_ref, cw_ref, cb_ref, wg_ref, ba_ref, bx_ref, lam_ref,
         h0_ref, hb_ref, wgate_ref, wout_ref, gpost_ref, o_ref, carry_out_ref, a_scr, b_scr, carry_ref) = refs
    else:
        (x_ref, xp_ref, xn_ref, mod_ref, gpre_ref, wr_ref, cw_ref, cb_ref, wg_ref, ba_ref, bx_ref, lam_ref,
         h0_ref, o_ref, carry_out_ref, a_scr, b_scr, carry_ref) = refs
    j = pl.program_id(1)
    tile = (nt - 1 - j) if reverse else j
    seg = tm // SUBLANES
    pitch = _seg_pitch(seg)

    @pl.when(j == 0)
    def _():
        for hd in range(RG_HEADS):
            carry_ref[hd] = h0_ref[0, 0:1, hd * RG_BLOCK:(hd + 1) * RG_BLOCK]

    shift, scale, gate = mod_ref[0, 0:1, :], mod_ref[0, 1:2, :], mod_ref[0, 2:3, :]
    x, hext = _prep_ext(x_ref, xp_ref, xn_ref, gpre_ref[...], shift, scale, tile, nt)
    rec = jnp.dot(hext.astype(BF16), wr_ref[...], preferred_element_type=F32)
    cw = cw_ref[...]
    u = cb_ref[...] + sum(cw[k:k + 1] * _shift_rows(rec, k - RG_CONV // 2, tm) for k in range(RG_CONV))

    lam = lam_ref[...]
    sp = jnp.maximum(-lam, 0.0) + jnp.log(1.0 + jnp.exp(-jnp.abs(lam)))
    for hd in range(RG_HEADS):
        ln = slice(hd * RG_BLOCK, (hd + 1) * RG_BLOCK)
        uh = u[:, ln]
        gz = jnp.dot(uh.astype(BF16), wg_ref[hd], preferred_element_type=F32)
        r = _sigmoid(gz[:, :RG_BLOCK] + ba_ref[:, ln])
        i = _sigmoid(gz[:, RG_BLOCK:] + bx_ref[:, ln])
        a = jnp.exp(-RG_C * r * sp[:, ln])
        bb = jnp.sqrt(1.0 - a * a) * (i * uh)
        for s in range(SUBLANES):
            a_scr[hd, s * pitch:s * pitch + seg, :] = a[s * seg:(s + 1) * seg]
            b_scr[hd, s * pitch:s * pitch + seg, :] = bb[s * seg:(s + 1) * seg]

    _scan_tile(a_scr, b_scr, carry_ref, seg=seg, reverse=reverse)

    def head_h(hd):
        return jnp.concatenate([b_scr[hd, s * pitch:s * pitch + seg, :] for s in range(SUBLANES)], axis=0)

    if final:
        g = _gelu(jnp.dot(hext[HALO:HALO + tm].astype(BF16), wgate_ref[...], preferred_element_type=F32))
        y = jnp.concatenate([head_h(hd) for hd in range(RG_HEADS)], axis=1) + hb_ref[0]
        out = jnp.dot((g * y).astype(BF16), wout_ref[...], preferred_element_type=F32)
        o_ref[0] = x + gate * _rms(out, gpost_ref[...])
    else:
        for hd in range(RG_HEADS):
            o_ref[0, :, hd * RG_BLOCK:(hd + 1) * RG_BLOCK] = head_h(hd)

    @pl.when(j == nt - 1)
    def _():
        for hd in range(RG_HEADS):
            carry_out_ref[0, :, hd * RG_BLOCK:(hd + 1) * RG_BLOCK] = jnp.broadcast_to(
                carry_ref[hd], (SUBLANES, RG_BLOCK))


def _rg_pass(x, mod, g_pre, w_rec, conv_w, conv_b, w_gates, b_a, b_x, lam, h0, *, tm, reverse,
             final=None):
    b, t, d = x.shape
    nt = t // tm
    seg = tm // SUBLANES
    main, prev, nxt = _tile_specs(tm, nt, reverse)
    carry_spec = pl.BlockSpec((1, SUBLANES, d), lambda bb, j: (bb, 0, 0))
    in_specs = [main, prev, nxt, _mod_spec(mod.shape[0]), _const_spec((1, d)), _const_spec(w_rec.shape),
                _const_spec(conv_w.shape), _const_spec((1, d)), _const_spec(w_gates.shape),
                _const_spec((1, d)), _const_spec((1, d)), _const_spec((1, d)), carry_spec]
    args = [x, x, x, mod, g_pre, w_rec, conv_w, conv_b, w_gates, b_a, b_x, lam, h0]
    if final is not None:
        h_other, w_gate, w_out, g_post = final
        in_specs += [main, _const_spec(w_gate.shape), _const_spec(w_out.shape), _const_spec((1, d))]
        args += [h_other, w_gate, w_out, g_post]
    kern = functools.partial(_rg_kernel, tm=tm, nt=nt, reverse=reverse, final=final is not None)
    scan_rows = SUBLANES * _seg_pitch(seg)
    return pl.pallas_call(
        kern,
        grid=(b, nt),
        in_specs=in_specs,
        out_specs=[main, carry_spec],
        out_shape=[jax.ShapeDtypeStruct(x.shape, F32), jax.ShapeDtypeStruct((b, SUBLANES, d), F32)],
        scratch_shapes=[pltpu.VMEM((RG_HEADS, scan_rows, RG_BLOCK), F32),
                        pltpu.VMEM((RG_HEADS, scan_rows, RG_BLOCK), F32),
                        pltpu.VMEM((RG_HEADS, 1, RG_BLOCK), F32)],
        compiler_params=pltpu.CompilerParams(dimension_semantics=("arbitrary", "arbitrary"),
                                             vmem_limit_bytes=VMEM_LIMIT),
        name="rg_final" if final is not None else "rg_scan",
    )(*args)


def _pool_kernel(*refs, tm, nt, width, halo_rows):
    if halo_rows:
        (x_ref, xp_ref, xn_ref, mod_ref, gpre_ref, gpost_ref, band_ref, invcc_ref, rowinv_ref, pw_ref, pb_ref,
         ps_ref, o_ref, hext_scr, rs_scr) = refs
    else:
        (x_ref, mod_ref, gpre_ref, gpost_ref, band_ref, invcc_ref, rowinv_ref, pw_ref, pb_ref, ps_ref,
         o_ref, hext_scr, rs_scr) = refs
    j = pl.program_id(1)
    rt = tm // width
    shift, scale, gate = mod_ref[0, 0:1, :], mod_ref[0, 1:2, :], mod_ref[0, 2:3, :]
    g_pre = gpre_ref[...]

    def prep(v):
        return _rms(v, g_pre) * (1.0 + scale) + shift

    x = x_ref[0]
    base = halo_rows * width
    hext_scr[base:base + tm] = prep(x)
    if halo_rows:
        hext_scr[0:base] = jnp.where(j > 0, prep(xp_ref[0]), 0.0)
        hext_scr[base + tm:base + tm + base] = jnp.where(j < nt - 1, prep(xn_ref[0]), 0.0)

    def row_body(r, _):
        for g, w in enumerate(POOL_WINDOWS):
            ln = slice(g * POOL_GROUP, (g + 1) * POOL_GROUP)
            offs = range(-(w // 2), w // 2) if halo_rows else range(0, 1)
            acc = None
            for dr in offs:
                start = pl.multiple_of((r + halo_rows + dr) * width, width)
                v = hext_scr[pl.ds(start, width), ln]
                acc = v if acc is None else acc + v
            inv = rowinv_ref[pl.ds(j * rt + r, 1), ln]
            rs_scr[pl.ds(pl.multiple_of(r * width, width), width), ln] = acc * inv
        return 0

    lax.fori_loop(0, rt, row_body, 0)

    for ch in range(tm // POOL_CHUNK):
        rows = slice(ch * POOL_CHUNK, (ch + 1) * POOL_CHUNK)
        for g in range(len(POOL_WINDOWS)):
            ln = slice(g * POOL_GROUP, (g + 1) * POOL_GROUP)
            rs = rs_scr[rows, ln]
            hi = rs.astype(BF16)
            lo = (rs - hi.astype(F32)).astype(BF16)
            pooled = (jnp.dot(band_ref[g], hi, preferred_element_type=F32)
                      + jnp.dot(band_ref[g], lo, preferred_element_type=F32)) * invcc_ref[:, ln]
            hg = hext_scr[base + ch * POOL_CHUNK:base + (ch + 1) * POOL_CHUNK, ln]
            yg = jnp.dot((pooled - hg).astype(BF16), pw_ref[g], preferred_element_type=F32)
            rs_scr[rows, ln] = (yg + pb_ref[:, ln]) * ps_ref[:, ln]

    o_ref[0] = x + gate * _rms(rs_scr[...], gpost_ref[...])


def _pool_consts(n_rows, width):
    idx = np.arange(POOL_CHUNK)
    row, col = idx // width, idx % width
    band = np.zeros((len(POOL_WINDOWS), POOL_CHUNK, POOL_CHUNK), np.float32)
    invcc = np.zeros((POOL_CHUNK, D_MODEL), np.float32)
    rowinv = np.zeros((n_rows, D_MODEL), np.float32)
    r = np.arange(n_rows)
    for g, w in enumerate(POOL_WINDOWS):
        same = row[:, None] == row[None, :]
        d = col[None, :] - col[:, None]
        band[g] = (same & (d >= -(w // 2)) & (d < w // 2)).astype(np.float32)
        cc = np.minimum(col + w // 2, width) - np.maximum(col - w // 2, 0)
        invcc[:, g * POOL_GROUP:(g + 1) * POOL_GROUP] = (1.0 / cc)[:, None]
        cr = np.minimum(r + w // 2, n_rows) - np.maximum(r - w // 2, 0)
        rowinv[:, g * POOL_GROUP:(g + 1) * POOL_GROUP] = (1.0 / cr)[:, None]
    return jnp.asarray(band, BF16), jnp.asarray(invcc), jnp.asarray(rowinv)


def _pool(x, mod, g_pre, g_post, pw, pb, ps, *, width, tm):
    b, t, d = x.shape
    n_rows = t // width
    nt = t // tm
    halo_rows = max(POOL_WINDOWS) // 2 if nt > 1 else 0
    assert nt > 1 or n_rows == 1
    band, invcc, rowinv = _pool_consts(n_rows, width)
    main = pl.BlockSpec((1, tm, d), lambda bb, j: (bb, j, 0))
    in_specs, args = [main], [x]
    if halo_rows:
        hb = halo_rows * width
        per = tm // hb
        last = t // hb - 1
        in_specs += [pl.BlockSpec((1, hb, d), lambda bb, j: (bb, jnp.maximum(j * per - 1, 0), 0)),
                     pl.BlockSpec((1, hb, d), lambda bb, j: (bb, jnp.minimum((j + 1) * per, last), 0))]
        args += [x, x]
    in_specs += [_mod_spec(mod.shape[0]), _const_spec((1, d)), _const_spec((1, d)), _const_spec(band.shape),
                 _const_spec(invcc.shape), _const_spec(rowinv.shape), _const_spec(pw.shape),
                 _const_spec((1, d)), _const_spec((1, d))]
    args += [mod, g_pre, g_post, band, invcc, rowinv, pw, pb, ps]
    kern = functools.partial(_pool_kernel, tm=tm, nt=nt, width=width, halo_rows=halo_rows)
    return pl.pallas_call(
        kern,
        grid=(b, nt),
        in_specs=in_specs,
        out_specs=main,
        out_shape=jax.ShapeDtypeStruct(x.shape, F32),
        scratch_shapes=[pltpu.VMEM((tm + 2 * halo_rows * width, d), F32), pltpu.VMEM((tm, d), F32)],
        compiler_params=pltpu.CompilerParams(dimension_semantics=("parallel", "parallel"),
                                             vmem_limit_bytes=VMEM_LIMIT),
        name="pool_mix",
    )(*args)


def kernel(x, c, ctx, c_ctx, ada_w, ada_b, norm_pre_mix, norm_post_mix, norm_pre_ffn, norm_post_ffn, rg_w_in,
           rg_conv_w, rg_conv_b, rg_gate_a_w, rg_gate_a_b, rg_gate_x_w, rg_gate_x_b, rg_lambda, rg_w_out, pool_w,
           pool_b, pool_scale, ffn_w_up, ffn_conv_w, ffn_conv_b, ffn_w_down):
    bsz, seq, d = x.shape
    ctx_len = ctx.shape[1]
    depth = ada_w.shape[0]
    assert d == D_MODEL and seq % GRID_W == 0

    ct = jnp.zeros((d, SUBLANES), F32).at[:, :bsz].set(c.T).at[:, bsz].set(c_ctx)
    mods = _ada(ct, ada_w, ada_b, bsz + 1)
    row = lambda a, i: a[i].reshape(1, -1)
    tm_lat = 512
    zeros_state = jnp.zeros((bsz, SUBLANES, d), F32)

    for i in range(depth):
        last = i == depth - 1
        mod_l = mods[i, :bsz].reshape(bsz, N_MOD, d)
        mod_c = mods[i, bsz].reshape(1, N_MOD, d)
        g_pre, g_post = row(norm_pre_mix, i), row(norm_post_mix, i)
        jm = i // 2
        if i % 2 == 0:
            w_gate = rg_w_in[jm, :, :d].astype(BF16)
            w_rec = rg_w_in[jm, :, d:].astype(BF16)
            w_out = rg_w_out[jm].astype(BF16)
            conv_w, conv_b = rg_conv_w[jm], row(rg_conv_b, jm)
            dirs = []
            for dd in range(2):
                wg = jnp.concatenate([rg_gate_a_w[jm, dd], rg_gate_x_w[jm, dd]], axis=-1).astype(BF16)
                dirs.append((wg, rg_gate_a_b[jm, dd].reshape(1, d), rg_gate_x_b[jm, dd].reshape(1, d),
                             rg_lambda[jm, dd].reshape(1, d)))
            common_c = (ctx, mod_c, g_pre, w_rec, conv_w, conv_b)
            common_l = (x, mod_l, g_pre, w_rec, conv_w, conv_b)
            hb_c, end_b = _rg_pass(*common_c, *dirs[1], zeros_state, tm=ctx_len, reverse=True)
            ctx_new, end_f = _rg_pass(*common_c, *dirs[0], zeros_state, tm=ctx_len, reverse=False,
                                      final=(hb_c, w_gate, w_out, g_post))
            hb_l, _ = _rg_pass(*common_l, *dirs[1], end_b, tm=tm_lat, reverse=True)
            x, _ = _rg_pass(*common_l, *dirs[0], end_f, tm=tm_lat, reverse=False,
                            final=(hb_l, w_gate, w_out, g_post))
            if not last:
                ctx = ctx_new
        else:
            pw = pool_w[jm].astype(BF16)
            pb, ps = row(pool_b, jm), row(pool_scale, jm)
            x_new = _pool(x, mod_l, g_pre, g_post, pw, pb, ps, width=GRID_W, tm=1024)
            if not last:
                ctx = _pool(ctx, mod_c, g_pre, g_post, pw, pb, ps, width=ctx_len, tm=ctx_len)
            x = x_new

        g_pre, g_post = row(norm_pre_ffn, i), row(norm_post_ffn, i)
        w_up, w_dn = ffn_w_up[i].astype(BF16), ffn_w_down[i].astype(BF16)
        cw, cb = ffn_conv_w[i], row(ffn_conv_b, i)
        x = _ffn(x, mod_l, g_pre, g_post, w_up, cw, cb, w_dn, tm=tm_lat)
        if not last:
            ctx = _ffn(ctx, mod_c, g_pre, g_post, w_up, cw, cb, w_dn, tm=ctx_len)
    return x
```

```python
import functools
import math

import numpy as np
import jax
import jax.numpy as jnp
from jax import lax
from jax.experimental import pallas as pl
from jax.experimental.pallas import tpu as pltpu

D_MODEL = 1024
GRID_W = 64
N_MOD = 6
RG_HEADS = 8
RG_BLOCK = D_MODEL // RG_HEADS
RG_CONV = 4
RG_C = 8.0
POOL_WINDOWS = (2, 4, 8, 16)
POOL_GROUP = D_MODEL // len(POOL_WINDOWS)
D_FF = 3 * D_MODEL
EPS = 1e-6

SUBLANES = 8
LANES = 128
HALO = 8
POOL_CHUNK = 256
VMEM_LIMIT = 56 * 1024 * 1024

F32 = jnp.float32
BF16 = jnp.bfloat16


def _norm_mod(v, gm, shift):
    r = lax.rsqrt(jnp.mean(v * v, axis=-1, keepdims=True) + EPS)
    y = (v * r) * gm
    return y if shift is None else y + shift


def _gelu(x):
    k1 = math.sqrt(2.0 / math.pi)
    hx = 0.5 * x
    return hx + hx * jnp.tanh(x * (k1 + (k1 * 0.044715) * (x * x)))


def _sigmoid(x):
    return 1.0 / (1.0 + jnp.exp(-x))


def _const_spec(shape):
    nd = len(shape)
    return pl.BlockSpec(shape, lambda *_: (0,) * nd, pipeline_mode=pl.Buffered(1))


def _tile_specs(tm, nt, reverse):
    per = tm // HALO
    last = nt * per - 1

    def tile(j):
        return (nt - 1 - j) if reverse else j

    main = pl.BlockSpec((1, tm, D_MODEL), lambda b, j: (b, tile(j), 0))
    prev = pl.BlockSpec((1, HALO, D_MODEL), lambda b, j: (b, jnp.maximum(tile(j) * per - 1, 0), 0))
    nxt = pl.BlockSpec((1, HALO, D_MODEL), lambda b, j: (b, jnp.minimum((tile(j) + 1) * per, last), 0))
    return main, prev, nxt


def _mod_spec(n_mod_rows):
    if n_mod_rows == 1:
        return pl.BlockSpec((1, N_MOD, D_MODEL), lambda b, j: (0, 0, 0))
    return pl.BlockSpec((1, N_MOD, D_MODEL), lambda b, j: (b, 0, 0))


def _ada_kernel(ct_ref, w_ref, b_ref, o_ref, *, n_rows):
    ct = ct_ref[...]
    st = ct * _sigmoid(ct)
    w = w_ref[0]
    rows = [jnp.sum(w * st[:, m:m + 1], axis=0, keepdims=True) for m in range(n_rows)]
    rows += [jnp.zeros_like(rows[0])] * (SUBLANES - n_rows)
    o_ref[0] = jnp.concatenate(rows, axis=0) + b_ref[0]


def _ada(ct, ada_w, ada_b, n_rows):
    depth, d, n = ada_w.shape
    tn = 1536
    return pl.pallas_call(
        functools.partial(_ada_kernel, n_rows=n_rows),
        grid=(depth, n // tn),
        in_specs=[pl.BlockSpec((d, SUBLANES), lambda i, k: (0, 0)),
                  pl.BlockSpec((1, d, tn), lambda i, k: (i, 0, k)),
                  pl.BlockSpec((1, 1, tn), lambda i, k: (i, 0, k))],
        out_specs=pl.BlockSpec((1, SUBLANES, tn), lambda i, k: (i, 0, k)),
        out_shape=jax.ShapeDtypeStruct((depth, SUBLANES, n), F32),
        compiler_params=pltpu.CompilerParams(dimension_semantics=("parallel", "parallel"),
                                             vmem_limit_bytes=VMEM_LIMIT),
        name="ada_mod",
    )(ct, ada_w, ada_b.reshape(depth, 1, n))


def _prep_ext(x_ref, xp_ref, xn_ref, g_pre, shift, scale, tile, nt):
    gm = g_pre * (1.0 + scale)

    def prep(v):
        return _norm_mod(v, gm, shift)

    hp = jnp.where(tile > 0, prep(xp_ref[0]), 0.0)
    hn = jnp.where(tile < nt - 1, prep(xn_ref[0]), 0.0)
    return jnp.concatenate([hp, prep(x_ref[0]), hn], axis=0)


def _shifted(slab_ref, slab, k, tm):
    if k == 0:
        return slab_ref[slab, pl.ds(HALO, tm), :]
    return slab_ref[slab, pl.ds(HALO + k, tm, stride=1), :]


def _ffn_kernel(x_ref, xp_ref, xn_ref, mod_ref, gpre_ref, gpost_ref, wup_ref, cw_ref, cb_ref, wdn_ref,
                o_ref, hext_ref, hm_ref, zs_ref, acc_ref, *, tm, nt, fc):
    j = pl.program_id(1)
    shift, scale, gate = mod_ref[0, 3:4, :], mod_ref[0, 4:5, :], mod_ref[0, 5:6, :]
    hext = _prep_ext(x_ref, xp_ref, xn_ref, gpre_ref[...], shift, scale, j, nt)
    hext_ref[...] = hext.astype(BF16)
    hm_ref[...] = hext[HALO:HALO + tm].astype(BF16)
    ns = fc // LANES
    for c in range(D_FF // fc):
        lo = c * fc
        base = (c % 2) * ns
        zu = jnp.dot(hext_ref[...], wup_ref[:, lo:lo + fc], preferred_element_type=F32)
        for l in range(ns):
            zs_ref[base + l] = zu[:, l * LANES:(l + 1) * LANES]
        zv = jnp.dot(hm_ref[...], wup_ref[:, D_FF + lo:D_FF + lo + fc], preferred_element_type=F32)
        acts = []
        for l in range(ns):
            ln = slice(lo + l * LANES, lo + (l + 1) * LANES)
            cw = cw_ref[:, ln]
            u = (cw[0:1] * _shifted(zs_ref, base + l, -1, tm) + cw[1:2] * _shifted(zs_ref, base + l, 0, tm)
                 + cw[2:3] * _shifted(zs_ref, base + l, 1, tm) + cb_ref[:, ln])
            acts.append((_gelu(u) * zv[:, l * LANES:(l + 1) * LANES]).astype(BF16))
        part = jnp.dot(jnp.concatenate(acts, axis=1), wdn_ref[lo:lo + fc, :], preferred_element_type=F32)
        if c == 0:
            acc_ref[...] = part
        else:
            acc_ref[...] += part
    o_ref[0] = x_ref[0] + _norm_mod(acc_ref[...], gate * gpost_ref[...], None)


def _ffn(x, mod, g_pre, g_post, w_up, conv_w, conv_b, w_down, *, tm, fc=512):
    b, t, d = x.shape
    nt = t // tm
    main, prev, nxt = _tile_specs(tm, nt, False)
    kern = functools.partial(_ffn_kernel, tm=tm, nt=nt, fc=fc)
    return pl.pallas_call(
        kern,
        grid=(b, nt),
        in_specs=[main, prev, nxt, _mod_spec(mod.shape[0]),
                  _const_spec((1, d)), _const_spec((1, d)),
                  _const_spec(w_up.shape), _const_spec(conv_w.shape), _const_spec((1, D_FF)),
                  _const_spec(w_down.shape)],
        out_specs=pl.BlockSpec((1, tm, d), lambda bb, j: (bb, j, 0)),
        out_shape=jax.ShapeDtypeStruct(x.shape, F32),
        scratch_shapes=[pltpu.VMEM((tm + 2 * HALO, d), BF16), pltpu.VMEM((tm, d), BF16),
                        pltpu.VMEM((2 * fc // LANES, tm + 2 * HALO, LANES), F32), pltpu.VMEM((tm, d), F32)],
        compiler_params=pltpu.CompilerParams(dimension_semantics=("parallel", "parallel"),
                                             vmem_limit_bytes=VMEM_LIMIT),
        name="conv_ffn",
    )(x, x, x, mod, g_pre, g_post, w_up, conv_w, conv_b, w_down)


def _seg_pitch(seg):
    p = seg + 4
    assert p % 8 == 4, seg
    return p


def _scan_tile(a_scr, b_scr, carry_ref, *, seg, reverse):
    pitch = _seg_pitch(seg)
    nh = a_scr.shape[0]
    sub = lax.broadcasted_iota(jnp.int32, (SUBLANES, RG_BLOCK), 0)

    def rows(i):
        return pl.ds((seg - 1 - i) if reverse else i, SUBLANES, stride=pitch)

    def pass1(i, st):
        out = []
        for hd in range(nh):
            av = a_scr[hd, rows(i), :]
            out += [av * st[2 * hd] + b_scr[hd, rows(i), :], st[2 * hd + 1] * av]
        return tuple(out)

    init = tuple(jnp.zeros((SUBLANES, RG_BLOCK), F32) if k % 2 == 0 else jnp.ones((SUBLANES, RG_BLOCK), F32)
                 for k in range(2 * nh))
    st = lax.fori_loop(0, seg, pass1, init, unroll=2)

    first = SUBLANES - 1 if reverse else 0
    last = 0 if reverse else SUBLANES - 1
    cins = []
    for hd in range(nh):
        e, pe = st[2 * hd], st[2 * hd + 1]
        carry = jnp.broadcast_to(carry_ref[hd], (SUBLANES, RG_BLOCK))
        cin = carry
        for _ in range(SUBLANES - 1):
            nxt = pltpu.roll(e + pe * cin, (SUBLANES - 1) if reverse else 1, axis=0)
            cin = jnp.where(sub == first, carry, nxt)
        cins.append(cin)
        end = e + pe * cin
        carry_ref[hd] = end[last:last + 1, :]

    def pass2(i, hs):
        out = []
        for hd in range(nh):
            h = a_scr[hd, rows(i), :] * hs[hd] + b_scr[hd, rows(i), :]
            b_scr[hd, rows(i), :] = h
            out.append(h)
        return tuple(out)

    lax.fori_loop(0, seg, pass2, tuple(cins), unroll=2)


def _rg_kernel(*refs, tm, nt, reverse, final):
    if final:
        (x_ref, xp_ref, xn_ref, mod_ref, gpre_ref, wr_ref, cw_ref, cb_ref, wg_ref, ba_ref, bx_ref, lam_ref,
         h0_ref, hb_ref, wgate_ref, wout_ref, gpost_ref, o_ref, carry_out_ref, rec_scr, a_scr, b_scr, carry_ref,
         hm_scr, gy_scr) = refs
    else:
        (x_ref, xp_ref, xn_ref, mod_ref, gpre_ref, wr_ref, cw_ref, cb_ref, wg_ref, ba_ref, bx_ref, lam_ref,
         h0_ref, o_ref, carry_out_ref, rec_scr, a_scr, b_scr, carry_ref) = refs
    j = pl.program_id(1)
    tile = (nt - 1 - j) if reverse else j
    seg = tm // SUBLANES
    pitch = _seg_pitch(seg)

    @pl.when(j == 0)
    def _():
        for hd in range(RG_HEADS):
            carry_ref[hd] = h0_ref[0, 0:1, hd * RG_BLOCK:(hd + 1) * RG_BLOCK]

    shift, scale, gate = mod_ref[0, 0:1, :], mod_ref[0, 1:2, :], mod_ref[0, 2:3, :]
    hext = _prep_ext(x_ref, xp_ref, xn_ref, gpre_ref[...], shift, scale, tile, nt)
    if final:
        hm_scr[...] = hext[HALO:HALO + tm].astype(BF16)
    rec = jnp.dot(hext.astype(BF16), wr_ref[...], preferred_element_type=F32)
    for hd in range(RG_HEADS):
        rec_scr[hd] = rec[:, hd * RG_BLOCK:(hd + 1) * RG_BLOCK]

    lam = lam_ref[...]
    sp = jnp.maximum(-lam, 0.0) + jnp.log(1.0 + jnp.exp(-jnp.abs(lam)))
    nc = (-0.5 * RG_C * math.log2(math.e)) * sp
    half_ba, half_bx = 0.5 * ba_ref[...], 0.5 * bx_ref[...]
    for hd in range(RG_HEADS):
        ln = slice(hd * RG_BLOCK, (hd + 1) * RG_BLOCK)
        u = cb_ref[:, ln] + sum(cw_ref[k:k + 1, ln] * _shifted(rec_scr, hd, k - RG_CONV // 2, tm)
                                for k in range(RG_CONV))
        uh = 0.5 * u
        gz = jnp.dot(uh.astype(BF16), wg_ref[hd], preferred_element_type=F32)
        tr = jnp.tanh(gz[:, :RG_BLOCK] + half_ba[:, ln])
        ti = jnp.tanh(gz[:, RG_BLOCK:] + half_bx[:, ln])
        a = jnp.exp2(nc[:, ln] + nc[:, ln] * tr)
        y1 = 1.0 - a * a
        mult = jnp.where(y1 > 0.0, y1 * lax.rsqrt(y1), 0.0)
        bb = mult * (uh + uh * ti)
        for s in range(SUBLANES):
            a_scr[hd, pl.ds(s * pitch, seg, stride=1), :] = a[s * seg:(s + 1) * seg]
            b_scr[hd, pl.ds(s * pitch, seg, stride=1), :] = bb[s * seg:(s + 1) * seg]

    _scan_tile(a_scr, b_scr, carry_ref, seg=seg, reverse=reverse)

    def head_h(hd):
        return jnp.concatenate([b_scr[hd, pl.ds(s * pitch, seg, stride=1), :] for s in range(SUBLANES)], axis=0)

    if final:
        g = _gelu(jnp.dot(hm_scr[...], wgate_ref[...], preferred_element_type=F32))
        for hd in range(RG_HEADS):
            ln = slice(hd * RG_BLOCK, (hd + 1) * RG_BLOCK)
            gy_scr[:, ln] = (g[:, ln] * (head_h(hd) + hb_ref[0, :, ln])).astype(BF16)
        out = jnp.dot(gy_scr[...], wout_ref[...], preferred_element_type=F32)
        o_ref[0] = x_ref[0] + _norm_mod(out, gate * gpost_ref[...], None)
    else:
        for hd in range(RG_HEADS):
            o_ref[0, :, hd * RG_BLOCK:(hd + 1) * RG_BLOCK] = head_h(hd)

    @pl.when(j == nt - 1)
    def _():
        for hd in range(RG_HEADS):
            carry_out_ref[0, :, hd * RG_BLOCK:(hd + 1) * RG_BLOCK] = jnp.broadcast_to(
                carry_ref[hd], (SUBLANES, RG_BLOCK))


def _rg_pass(x, mod, g_pre, w_rec, conv_w, conv_b, w_gates, b_a, b_x, lam, h0, *, tm, reverse,
             final=None):
    b, t, d = x.shape
    nt = t // tm
    seg = tm // SUBLANES
    main, prev, nxt = _tile_specs(tm, nt, reverse)
    carry_spec = pl.BlockSpec((1, SUBLANES, d), lambda bb, j: (bb, 0, 0))
    in_specs = [main, prev, nxt, _mod_spec(mod.shape[0]), _const_spec((1, d)), _const_spec(w_rec.shape),
                _const_spec(conv_w.shape), _const_spec((1, d)), _const_spec(w_gates.shape),
                _const_spec((1, d)), _const_spec((1, d)), _const_spec((1, d)), carry_spec]
    args = [x, x, x, mod, g_pre, w_rec, conv_w, conv_b, w_gates, b_a, b_x, lam, h0]
    if final is not None:
        h_other, w_gate, w_out, g_post = final
        in_specs += [main, _const_spec(w_gate.shape), _const_spec(w_out.shape), _const_spec((1, d))]
        args += [h_other, w_gate, w_out, g_post]
    kern = functools.partial(_rg_kernel, tm=tm, nt=nt, reverse=reverse, final=final is not None)
    scan_rows = SUBLANES * _seg_pitch(seg)
    scratch = [pltpu.VMEM((RG_HEADS, tm + 2 * HALO, RG_BLOCK), F32),
               pltpu.VMEM((RG_HEADS, scan_rows, RG_BLOCK), F32),
               pltpu.VMEM((RG_HEADS, scan_rows, RG_BLOCK), F32),
               pltpu.VMEM((RG_HEADS, 1, RG_BLOCK), F32)]
    if final is not None:
        scratch += [pltpu.VMEM((tm, d), BF16), pltpu.VMEM((tm, d), BF16)]
    return pl.pallas_call(
        kern,
        grid=(b, nt),
        in_specs=in_specs,
        out_specs=[main, carry_spec],
        out_shape=[jax.ShapeDtypeStruct(x.shape, F32), jax.ShapeDtypeStruct((b, SUBLANES, d), F32)],
        scratch_shapes=scratch,
        compiler_params=pltpu.CompilerParams(dimension_semantics=("arbitrary", "arbitrary"),
                                             vmem_limit_bytes=VMEM_LIMIT),
        name="rg_final" if final is not None else "rg_scan",
    )(*args)


def _pool_kernel(*refs, tm, nt, width, halo_rows):
    if halo_rows:
        (x_ref, xp_ref, xn_ref, mod_ref, gpre_ref, gpost_ref, band_ref, invcc_ref, rowinv_ref, pw_ref, pb_ref,
         ps_ref, o_ref, hext_scr, rs_scr) = refs
    else:
        (x_ref, mod_ref, gpre_ref, gpost_ref, band_ref, invcc_ref, rowinv_ref, pw_ref, pb_ref, ps_ref,
         o_ref, hext_scr, rs_scr) = refs
    j = pl.program_id(1)
    rt = tm // width
    shift, scale, gate = mod_ref[0, 0:1, :], mod_ref[0, 1:2, :], mod_ref[0, 2:3, :]
    gm = gpre_ref[...] * (1.0 + scale)

    def prep(v):
        return _norm_mod(v, gm, shift)

    base = halo_rows * width
    hext_scr[base:base + tm] = prep(x_ref[0])
    if halo_rows:
        hext_scr[0:base] = jnp.where(j > 0, prep(xp_ref[0]), 0.0)
        hext_scr[base + tm:base + tm + base] = jnp.where(j < nt - 1, prep(xn_ref[0]), 0.0)

    def row_body(r, _):
        for g, w in enumerate(POOL_WINDOWS):
            ln = slice(g * POOL_GROUP, (g + 1) * POOL_GROUP)
            offs = range(-(w // 2), w // 2) if halo_rows else range(0, 1)
            acc = None
            for dr in offs:
                start = pl.multiple_of((r + halo_rows + dr) * width, width)
                v = hext_scr[pl.ds(start, width), ln]
                acc = v if acc is None else acc + v
            inv = rowinv_ref[pl.ds(j * rt + r, 1), ln]
            rs_scr[pl.ds(pl.multiple_of(r * width, width), width), ln] = acc * inv
        return 0

    lax.fori_loop(0, rt, row_body, 0)

    for ch in range(tm // POOL_CHUNK):
        rows = slice(ch * POOL_CHUNK, (ch + 1) * POOL_CHUNK)
        for g in range(len(POOL_WINDOWS)):
            ln = slice(g * POOL_GROUP, (g + 1) * POOL_GROUP)
            rs = rs_scr[rows, ln]
            hi = rs.astype(BF16)
            lo = (rs - hi.astype(F32)).astype(BF16)
            pooled = (jnp.dot(band_ref[g], hi, preferred_element_type=F32)
                      + jnp.dot(band_ref[g], lo, preferred_element_type=F32)) * invcc_ref[:, ln]
            hg = hext_scr[base + ch * POOL_CHUNK:base + (ch + 1) * POOL_CHUNK, ln]
            yg = jnp.dot((pooled - hg).astype(BF16), pw_ref[g], preferred_element_type=F32)
            rs_scr[rows, ln] = (yg + pb_ref[:, ln]) * ps_ref[:, ln]

    o_ref[0] = x_ref[0] + _norm_mod(rs_scr[...], gate * gpost_ref[...], None)


def _pool_consts(n_rows, width):
    idx = np.arange(POOL_CHUNK)
    row, col = idx // width, idx % width
    band = np.zeros((len(POOL_WINDOWS), POOL_CHUNK, POOL_CHUNK), np.float32)
    invcc = np.zeros((POOL_CHUNK, D_MODEL), np.float32)
    rowinv = np.zeros((n_rows, D_MODEL), np.float32)
    r = np.arange(n_rows)
    for g, w in enumerate(POOL_WINDOWS):
        same = row[:, None] == row[None, :]
        d = col[None, :] - col[:, None]
        band[g] = (same & (d >= -(w // 2)) & (d < w // 2)).astype(np.float32)
        cc = np.minimum(col + w // 2, width) - np.maximum(col - w // 2, 0)
        invcc[:, g * POOL_GROUP:(g + 1) * POOL_GROUP] = (1.0 / cc)[:, None]
        cr = np.minimum(r + w // 2, n_rows) - np.maximum(r - w // 2, 0)
        rowinv[:, g * POOL_GROUP:(g + 1) * POOL_GROUP] = (1.0 / cr)[:, None]
    return jnp.asarray(band, BF16), jnp.asarray(invcc), jnp.asarray(rowinv)


def _pool(x, mod, g_pre, g_post, pw, pb, ps, *, width, tm):
    b, t, d = x.shape
    n_rows = t // width
    nt = t // tm
    halo_rows = max(POOL_WINDOWS) // 2 if nt > 1 else 0
    assert nt > 1 or n_rows == 1
    band, invcc, rowinv = _pool_consts(n_rows, width)
    main = pl.BlockSpec((1, tm, d), lambda bb, j: (bb, j, 0))
    in_specs, args = [main], [x]
    if halo_rows:
        hb = halo_rows * width
        per = tm // hb
        last = t // hb - 1
        in_specs += [pl.BlockSpec((1, hb, d), lambda bb, j: (bb, jnp.maximum(j * per - 1, 0), 0)),
                     pl.BlockSpec((1, hb, d), lambda bb, j: (bb, jnp.minimum((j + 1) * per, last), 0))]
        args += [x, x]
    in_specs += [_mod_spec(mod.shape[0]), _const_spec((1, d)), _const_spec((1, d)), _const_spec(band.shape),
                 _const_spec(invcc.shape), _const_spec(rowinv.shape), _const_spec(pw.shape),
                 _const_spec((1, d)), _const_spec((1, d))]
    args += [mod, g_pre, g_post, band, invcc, rowinv, pw, pb, ps]
    kern = functools.partial(_pool_kernel, tm=tm, nt=nt, width=width, halo_rows=halo_rows)
    return pl.pallas_call(
        kern,
        grid=(b, nt),
        in_specs=in_specs,
        out_specs=main,
        out_shape=jax.ShapeDtypeStruct(x.shape, F32),
        scratch_shapes=[pltpu.VMEM((tm + 2 * halo_rows * width, d), F32), pltpu.VMEM((tm, d), F32)],
        compiler_params=pltpu.CompilerParams(dimension_semantics=("parallel", "parallel"),
                                             vmem_limit_bytes=VMEM_LIMIT),
        name="pool_mix",
    )(*args)


def kernel(x, c, ctx, c_ctx, ada_w, ada_b, norm_pre_mix, norm_post_mix, norm_pre_ffn, norm_post_ffn, rg_w_in,
           rg_conv_w, rg_conv_b, rg_gate_a_w, rg_gate_a_b, rg_gate_x_w, rg_gate_x_b, rg_lambda, rg_w_out, pool_w,
           pool_b, pool_scale, ffn_w_up, ffn_conv_w, ffn_conv_b, ffn_w_down):
    bsz, seq, d = x.shape
    ctx_len = ctx.shape[1]
    depth = ada_w.shape[0]
    assert d == D_MODEL and seq % GRID_W == 0

    ct = jnp.zeros((d, SUBLANES), F32).at[:, :bsz].set(c.T).at[:, bsz].set(c_ctx)
    mods = _ada(ct, ada_w, ada_b, bsz + 1)
    row = lambda a, i: a[i].reshape(1, -1)
    tm_lat = 512
    zeros_state = jnp.zeros((bsz, SUBLANES, d), F32)

    for i in range(depth):
        last = i == depth - 1
        mod_l = mods[i, :bsz].reshape(bsz, N_MOD, d)
        mod_c = mods[i, bsz].reshape(1, N_MOD, d)
        g_pre, g_post = row(norm_pre_mix, i), row(norm_post_mix, i)
        jm = i // 2
        if i % 2 == 0:
            w_gate = rg_w_in[jm, :, :d].astype(BF16)
            w_rec = rg_w_in[jm, :, d:].astype(BF16)
            w_out = rg_w_out[jm].astype(BF16)
            conv_w, conv_b = rg_conv_w[jm], row(rg_conv_b, jm)
            dirs = []
            for dd in range(2):
                wg = jnp.concatenate([rg_gate_a_w[jm, dd], rg_gate_x_w[jm, dd]], axis=-1).astype(BF16)
                dirs.append((wg, rg_gate_a_b[jm, dd].reshape(1, d), rg_gate_x_b[jm, dd].reshape(1, d),
                             rg_lambda[jm, dd].reshape(1, d)))
            common_c = (ctx, mod_c, g_pre, w_rec, conv_w, conv_b)
            common_l = (x, mod_l, g_pre, w_rec, conv_w, conv_b)
            hb_c, end_b = _rg_pass(*common_c, *dirs[1], zeros_state, tm=ctx_len, reverse=True)
            ctx_new, end_f = _rg_pass(*common_c, *dirs[0], zeros_state, tm=ctx_len, reverse=False,
                                      final=(hb_c, w_gate, w_out, g_post))
            hb_l, _ = _rg_pass(*common_l, *dirs[1], end_b, tm=tm_lat, reverse=True)
            x, _ = _rg_pass(*common_l, *dirs[0], end_f, tm=tm_lat, reverse=False,
                            final=(hb_l, w_gate, w_out, g_post))
            if not last:
                ctx = ctx_new
        else:
            pw = pool_w[jm].astype(BF16)
            pb, ps = row(pool_b, jm), row(pool_scale, jm)
            x_new = _pool(x, mod_l, g_pre, g_post, pw, pb, ps, width=GRID_W, tm=1024)
            if not last:
                ctx = _pool(ctx, mod_c, g_pre, g_post, pw, pb, ps, width=ctx_len, tm=ctx_len)
            x = x_new

        g_pre, g_post = row(norm_pre_ffn, i), row(norm_post_ffn, i)
        w_up, w_dn = ffn_w_up[i].astype(BF16), ffn_w_down[i].astype(BF16)
        cw, cb = ffn_conv_w[i], row(ffn_conv_b, i)
        x = _ffn(x, mod_l, g_pre, g_post, w_up, cw, cb, w_dn, tm=tm_lat)
        if not last:
            ctx = _ffn(ctx, mod_c, g_pre, g_post, w_up, cw, cb, w_dn, tm=ctx_len)
    return x
```

```python
import functools
import math

import numpy as np
import jax
import jax.numpy as jnp
from jax import lax
from jax.experimental import pallas as pl
from jax.experimental.pallas import tpu as pltpu

D_MODEL = 1024
GRID_W = 64
N_MOD = 6
RG_HEADS = 8
RG_BLOCK = D_MODEL // RG_HEADS
RG_CONV = 4
RG_C = 8.0
POOL_WINDOWS = (2, 4, 8, 16)
POOL_GROUP = D_MODEL // len(POOL_WINDOWS)
D_FF = 3 * D_MODEL
EPS = 1e-6

SUBLANES = 8
LANES = 128
MXU_N = 512
HEADS_PER_TILE = MXU_N // RG_BLOCK
HALO = 8
POOL_CHUNK = 256
SCAN_SEGS = 16
VMEM_LIMIT = 56 * 1024 * 1024

F32 = jnp.float32
BF16 = jnp.bfloat16


def _norm_mod(v, gm, shift):
    r = lax.rsqrt(jnp.mean(v * v, axis=-1, keepdims=True) + EPS)
    y = (v * r) * gm
    return y if shift is None else y + shift


def _gelu(x):
    k1 = math.sqrt(2.0 / math.pi)
    hx = 0.5 * x
    return hx + hx * jnp.tanh(x * (k1 + (k1 * 0.044715) * (x * x)))


def _sigmoid(x):
    return 1.0 / (1.0 + jnp.exp(-x))


def _const_spec(shape):
    nd = len(shape)
    return pl.BlockSpec(shape, lambda *_: (0,) * nd, pipeline_mode=pl.Buffered(1))


def _tile_specs(tm, nt, reverse):
    per = tm // HALO
    last = nt * per - 1

    def tile(j):
        return (nt - 1 - j) if reverse else j

    main = pl.BlockSpec((1, tm, D_MODEL), lambda b, j: (b, tile(j), 0))
    prev = pl.BlockSpec((1, HALO, D_MODEL), lambda b, j: (b, jnp.maximum(tile(j) * per - 1, 0), 0))
    nxt = pl.BlockSpec((1, HALO, D_MODEL), lambda b, j: (b, jnp.minimum((tile(j) + 1) * per, last), 0))
    return main, prev, nxt


def _mod_spec(n_mod_rows):
    if n_mod_rows == 1:
        return pl.BlockSpec((1, N_MOD, D_MODEL), lambda b, j: (0, 0, 0))
    return pl.BlockSpec((1, N_MOD, D_MODEL), lambda b, j: (b, 0, 0))


def _ada_kernel(ct_ref, w_ref, b_ref, o_ref, *, n_rows):
    ct = ct_ref[...]
    st = ct * _sigmoid(ct)
    w = w_ref[0]
    rows = [jnp.sum(w * st[:, m:m + 1], axis=0, keepdims=True) for m in range(n_rows)]
    rows += [jnp.zeros_like(rows[0])] * (SUBLANES - n_rows)
    o_ref[0] = jnp.concatenate(rows, axis=0) + b_ref[0]


def _ada(ct, ada_w, ada_b, n_rows):
    depth, d, n = ada_w.shape
    tn = 1536
    return pl.pallas_call(
        functools.partial(_ada_kernel, n_rows=n_rows),
        grid=(depth, n // tn),
        in_specs=[pl.BlockSpec((d, SUBLANES), lambda i, k: (0, 0)),
                  pl.BlockSpec((1, d, tn), lambda i, k: (i, 0, k)),
                  pl.BlockSpec((1, 1, tn), lambda i, k: (i, 0, k))],
        out_specs=pl.BlockSpec((1, SUBLANES, tn), lambda i, k: (i, 0, k)),
        out_shape=jax.ShapeDtypeStruct((depth, SUBLANES, n), F32),
        compiler_params=pltpu.CompilerParams(dimension_semantics=("parallel", "parallel"),
                                             vmem_limit_bytes=VMEM_LIMIT),
        name="ada_mod",
    )(ct, ada_w, ada_b.reshape(depth, 1, n))


def _prep_ext(x_ref, xp_ref, xn_ref, g_pre, shift, scale, tile, nt):
    gm = g_pre * (1.0 + scale)

    def prep(v):
        return _norm_mod(v, gm, shift)

    hp = jnp.where(tile > 0, prep(xp_ref[0]), 0.0)
    hn = jnp.where(tile < nt - 1, prep(xn_ref[0]), 0.0)
    return jnp.concatenate([hp, prep(x_ref[0]), hn], axis=0)


def _shifted(slab_ref, slab, k, tm):
    if k == 0:
        return slab_ref[slab, pl.ds(HALO, tm), :]
    return slab_ref[slab, pl.ds(HALO + k, tm, stride=1), :]


def _ffn_kernel(x_ref, xp_ref, xn_ref, mod_ref, gpre_ref, gpost_ref, wup_ref, cw_ref, cb_ref, wdn_ref,
                o_ref, hext_ref, hm_ref, zs0_ref, zs1_ref, zv0_ref, zv1_ref, act_ref, *, tm, nt, fc):
    j = pl.program_id(1)
    shift, scale, gate = mod_ref[0, 3:4, :], mod_ref[0, 4:5, :], mod_ref[0, 5:6, :]
    hext = _prep_ext(x_ref, xp_ref, xn_ref, gpre_ref[...], shift, scale, j, nt)
    hext_ref[...] = hext.astype(BF16)
    hm_ref[...] = hext[HALO:HALO + tm].astype(BF16)
    ns = fc // LANES
    n_chunks = D_FF // fc
    zs_refs, zv_refs = (zs0_ref, zs1_ref), (zv0_ref, zv1_ref)

    def up(c):
        lo, zs_ref = c * fc, zs_refs[c % 2]
        zu = jnp.dot(hext_ref[...], wup_ref[:, lo:lo + fc], preferred_element_type=F32)
        for l in range(ns):
            zs_ref[l] = zu[:, l * LANES:(l + 1) * LANES]
        zv_refs[c % 2][...] = jnp.dot(hm_ref[...], wup_ref[:, D_FF + lo:D_FF + lo + fc],
                                      preferred_element_type=F32)

    def act(c):
        lo, zs_ref, zv_ref = c * fc, zs_refs[c % 2], zv_refs[c % 2]
        for l in range(ns):
            ln = slice(lo + l * LANES, lo + (l + 1) * LANES)
            cw = cw_ref[:, ln]
            u = (cw[0:1] * _shifted(zs_ref, l, -1, tm) + cw[1:2] * _shifted(zs_ref, l, 0, tm)
                 + cw[2:3] * _shifted(zs_ref, l, 1, tm) + cb_ref[:, ln])
            act_ref[:, ln] = (_gelu(u) * zv_ref[:, l * LANES:(l + 1) * LANES]).astype(BF16)

    up(0)
    for c in range(n_chunks):
        if c + 1 < n_chunks:
            up(c + 1)
        act(c)
    out = jnp.dot(act_ref[...], wdn_ref[...], preferred_element_type=F32)
    o_ref[0] = x_ref[0] + _norm_mod(out, gate * gpost_ref[...], None)


def _ffn(x, mod, g_pre, g_post, w_up, conv_w, conv_b, w_down, *, tm, fc=512):
    b, t, d = x.shape
    nt = t // tm
    main, prev, nxt = _tile_specs(tm, nt, False)
    kern = functools.partial(_ffn_kernel, tm=tm, nt=nt, fc=fc)
    return pl.pallas_call(
        kern,
        grid=(b, nt),
        in_specs=[main, prev, nxt, _mod_spec(mod.shape[0]),
                  _const_spec((1, d)), _const_spec((1, d)),
                  _const_spec(w_up.shape), _const_spec(conv_w.shape), _const_spec((1, D_FF)),
                  _const_spec(w_down.shape)],
        out_specs=pl.BlockSpec((1, tm, d), lambda bb, j: (bb, j, 0)),
        out_shape=jax.ShapeDtypeStruct(x.shape, F32),
        scratch_shapes=[pltpu.VMEM((tm + 2 * HALO, d), BF16), pltpu.VMEM((tm, d), BF16),
                        pltpu.VMEM((fc // LANES, tm + 2 * HALO, LANES), F32),
                        pltpu.VMEM((fc // LANES, tm + 2 * HALO, LANES), F32),
                        pltpu.VMEM((tm, fc), F32), pltpu.VMEM((tm, fc), F32), pltpu.VMEM((tm, D_FF), BF16)],
        compiler_params=pltpu.CompilerParams(dimension_semantics=("parallel", "parallel"),
                                             vmem_limit_bytes=VMEM_LIMIT),
        name="conv_ffn",
    )(x, x, x, mod, g_pre, g_post, w_up, conv_w, conv_b, w_down)


def _seg_pitch(seg):
    p = seg + 4
    assert p % 8 == 4, seg
    return p


def _scan_tile(a_scr, b_scr, h_scr, carry_ref, *, seg, reverse):
    pitch = _seg_pitch(seg)
    nh = a_scr.shape[0]
    nv = SCAN_SEGS // SUBLANES
    sub = lax.broadcasted_iota(jnp.int32, (SUBLANES, RG_BLOCK), 0)
    chains = [(hd, v) for hd in range(nh) for v in range(nv)]

    def rows(v, i):
        return pl.ds(v * SUBLANES * pitch + ((seg - 1 - i) if reverse else i), SUBLANES, stride=pitch)

    def pass1(i, st):
        out = []
        for k, (hd, v) in enumerate(chains):
            av = a_scr[hd, rows(v, i), :]
            out += [av * st[2 * k] + b_scr[hd, rows(v, i), :], st[2 * k + 1] * av]
        return tuple(out)

    init = tuple(jnp.zeros((SUBLANES, RG_BLOCK), F32) if k % 2 == 0 else jnp.ones((SUBLANES, RG_BLOCK), F32)
                 for k in range(2 * len(chains)))
    st = lax.fori_loop(0, seg, pass1, init, unroll=2)

    first = SUBLANES - 1 if reverse else 0
    last = 0 if reverse else SUBLANES - 1
    cins = {}
    for hd in range(nh):
        carry = jnp.broadcast_to(carry_ref[hd], (SUBLANES, RG_BLOCK))
        for v in (reversed(range(nv)) if reverse else range(nv)):
            k = hd * nv + v
            e, pe = st[2 * k], st[2 * k + 1]
            cin = carry
            for _ in range(SUBLANES - 1):
                nxt = pltpu.roll(e + pe * cin, (SUBLANES - 1) if reverse else 1, axis=0)
                cin = jnp.where(sub == first, carry, nxt)
            cins[(hd, v)] = cin
            end = e + pe * cin
            carry = jnp.broadcast_to(end[last:last + 1, :], (SUBLANES, RG_BLOCK))
        carry_ref[hd] = carry[0:1, :]

    def pass2(i, hs):
        out = []
        for k, (hd, v) in enumerate(chains):
            h = a_scr[hd, rows(v, i), :] * hs[k] + b_scr[hd, rows(v, i), :]
            h_scr[hd, rows(v, i), :] = h
            out.append(h)
        return tuple(out)

    lax.fori_loop(0, seg, pass2, tuple(cins[ch] for ch in chains), unroll=2)


def _rg_gates_scan(uh_of_head, after_gate_matmul, wg_ref, ba_ref, bx_ref, lam_ref, h0_ref, carry_out_ref,
                   a_scr, b_scr, h_scr, carry_ref, *, tm, nt, reverse):
    j = pl.program_id(1)
    seg = tm // SCAN_SEGS
    pitch = _seg_pitch(seg)

    @pl.when(j == 0)
    def _():
        for hd in range(RG_HEADS):
            carry_ref[hd] = h0_ref[0, 0:1, hd * RG_BLOCK:(hd + 1) * RG_BLOCK]

    lam = lam_ref[...]
    sp = jnp.maximum(-lam, 0.0) + jnp.log(1.0 + jnp.exp(-jnp.abs(lam)))
    nc = (-0.5 * RG_C * math.log2(math.e)) * sp
    half_ba, half_bx = 0.5 * ba_ref[...], 0.5 * bx_ref[...]
    for pair in range(RG_HEADS // 2):
        uhs = [uh_of_head(2 * pair + q) for q in range(2)]
        gz = jnp.dot(jnp.concatenate(uhs, axis=1).astype(BF16), wg_ref[pair], preferred_element_type=F32)
        after_gate_matmul(2 * pair + 1)
        for q in range(2):
            hd, uh = 2 * pair + q, uhs[q]
            ln = slice(hd * RG_BLOCK, (hd + 1) * RG_BLOCK)
            tr = jnp.tanh(gz[:, 2 * q * RG_BLOCK:(2 * q + 1) * RG_BLOCK] + half_ba[:, ln])
            ti = jnp.tanh(gz[:, (2 * q + 1) * RG_BLOCK:(2 * q + 2) * RG_BLOCK] + half_bx[:, ln])
            a = jnp.exp2(nc[:, ln] + nc[:, ln] * tr)
            y1 = 1.0 - a * a
            mult = jnp.where(y1 > 0.0, y1 * lax.rsqrt(y1), 0.0)
            bb = mult * (uh + uh * ti)
            for s in range(SCAN_SEGS):
                a_scr[hd, pl.ds(s * pitch, seg, stride=1), :] = a[s * seg:(s + 1) * seg]
                b_scr[hd, pl.ds(s * pitch, seg, stride=1), :] = bb[s * seg:(s + 1) * seg]

    _scan_tile(a_scr, b_scr, h_scr, carry_ref, seg=seg, reverse=reverse)

    @pl.when(j == nt - 1)
    def _():
        for hd in range(RG_HEADS):
            carry_out_ref[0, :, hd * RG_BLOCK:(hd + 1) * RG_BLOCK] = jnp.broadcast_to(
                carry_ref[hd], (SUBLANES, RG_BLOCK))


def _head_h(h_scr, hd, tm):
    seg = tm // SCAN_SEGS
    pitch = _seg_pitch(seg)
    return jnp.concatenate([h_scr[hd, pl.ds(s * pitch, seg, stride=1), :] for s in range(SCAN_SEGS)], axis=0)


def _rg_scan_kernel(x_ref, xp_ref, xn_ref, mod_ref, gpre_ref, wr_ref, cw_ref, cb_ref, wg_ref, ba_ref, bx_ref,
                    lam_ref, h0_ref, h_ref, uh_ref, carry_out_ref, hext_scr, rec0_scr, rec1_scr, a_scr, b_scr,
                    h_scr, carry_ref, *, tm, nt):
    tile = nt - 1 - pl.program_id(1)
    shift, scale = mod_ref[0, 0:1, :], mod_ref[0, 1:2, :]
    hext_scr[...] = _prep_ext(x_ref, xp_ref, xn_ref, gpre_ref[...], shift, scale, tile, nt).astype(BF16)
    half_cw, half_cb = 0.5 * cw_ref[...], 0.5 * cb_ref[...]

    rec_scrs = (rec0_scr, rec1_scr)
    assert len(rec_scrs) * HEADS_PER_TILE == RG_HEADS

    def rec_chunk(p):
        r = jnp.dot(hext_scr[...], wr_ref[:, p * MXU_N:(p + 1) * MXU_N], preferred_element_type=F32)
        for q in range(HEADS_PER_TILE):
            rec_scrs[p][q] = r[:, q * RG_BLOCK:(q + 1) * RG_BLOCK]

    rec_chunk(0)

    def after_gate_matmul(hd):
        p, q = divmod(hd, HEADS_PER_TILE)
        if q == 1 and p + 1 < len(rec_scrs):
            rec_chunk(p + 1)

    def uh_of_head(hd):
        p, q = divmod(hd, HEADS_PER_TILE)
        ln = slice(hd * RG_BLOCK, (hd + 1) * RG_BLOCK)
        uh = half_cb[:, ln] + sum(half_cw[k:k + 1, ln] * _shifted(rec_scrs[p], q, k - RG_CONV // 2, tm)
                                  for k in range(RG_CONV))
        uh_ref[0, :, ln] = uh
        return uh

    _rg_gates_scan(uh_of_head, after_gate_matmul, wg_ref, ba_ref, bx_ref, lam_ref, h0_ref, carry_out_ref,
                   a_scr, b_scr, h_scr, carry_ref, tm=tm, nt=nt, reverse=True)
    for hd in range(RG_HEADS):
        h_ref[0, :, hd * RG_BLOCK:(hd + 1) * RG_BLOCK] = _head_h(h_scr, hd, tm)


def _rg_final_kernel(x_ref, mod_ref, gpre_ref, uh_ref, wg_ref, ba_ref, bx_ref, lam_ref, h0_ref, hb_ref,
                     wgate_ref, wout_ref, gpost_ref, o_ref, carry_out_ref, a_scr, b_scr, h_scr, carry_ref,
                     hm_scr, gz_scr, gy_scr, *, tm, nt):
    shift, scale, gate = mod_ref[0, 0:1, :], mod_ref[0, 1:2, :], mod_ref[0, 2:3, :]
    hm_scr[...] = _norm_mod(x_ref[0], gpre_ref[...] * (1.0 + scale), shift).astype(BF16)

    def after_gate_matmul(hd):
        pass

    def uh_of_head(hd):
        if hd % 2 == 0:
            ln2 = slice(hd * RG_BLOCK, (hd + 2) * RG_BLOCK)
            gz_scr[:, ln2] = jnp.dot(hm_scr[...], wgate_ref[:, ln2], preferred_element_type=F32)
        return uh_ref[0, :, hd * RG_BLOCK:(hd + 1) * RG_BLOCK]

    _rg_gates_scan(uh_of_head, after_gate_matmul, wg_ref, ba_ref, bx_ref, lam_ref, h0_ref, carry_out_ref,
                   a_scr, b_scr, h_scr, carry_ref, tm=tm, nt=nt, reverse=False)
    for hd in range(RG_HEADS):
        ln = slice(hd * RG_BLOCK, (hd + 1) * RG_BLOCK)
        gy_scr[:, ln] = (_gelu(gz_scr[:, ln]) * (_head_h(h_scr, hd, tm) + hb_ref[0, :, ln])).astype(BF16)
    out = jnp.dot(gy_scr[...], wout_ref[...], preferred_element_type=F32)
    o_ref[0] = x_ref[0] + _norm_mod(out, gate * gpost_ref[...], None)


def _scan_scratch(tm):
    rows = SCAN_SEGS * _seg_pitch(tm // SCAN_SEGS)
    return [pltpu.VMEM((RG_HEADS, rows, RG_BLOCK), F32)] * 3 + [pltpu.VMEM((RG_HEADS, 1, RG_BLOCK), F32)]


def _rg_reverse(x, mod, g_pre, w_rec, conv_w, conv_b, w_gates, b_a, b_x, lam, h0, *, tm):
    b, t, d = x.shape
    nt = t // tm
    main, prev, nxt = _tile_specs(tm, nt, True)
    carry_spec = pl.BlockSpec((1, SUBLANES, d), lambda bb, j: (bb, 0, 0))
    return pl.pallas_call(
        functools.partial(_rg_scan_kernel, tm=tm, nt=nt),
        grid=(b, nt),
        in_specs=[main, prev, nxt, _mod_spec(mod.shape[0]), _const_spec((1, d)), _const_spec(w_rec.shape),
                  _const_spec(conv_w.shape), _const_spec((1, d)), _const_spec(w_gates.shape),
                  _const_spec((1, d)), _const_spec((1, d)), _const_spec((1, d)), carry_spec],
        out_specs=[main, main, carry_spec],
        out_shape=[jax.ShapeDtypeStruct(x.shape, F32), jax.ShapeDtypeStruct(x.shape, F32),
                   jax.ShapeDtypeStruct((b, SUBLANES, d), F32)],
        scratch_shapes=[pltpu.VMEM((tm + 2 * HALO, d), BF16),
                        pltpu.VMEM((HEADS_PER_TILE, tm + 2 * HALO, RG_BLOCK), F32),
                        pltpu.VMEM((HEADS_PER_TILE, tm + 2 * HALO, RG_BLOCK), F32)] + _scan_scratch(tm),
        compiler_params=pltpu.CompilerParams(dimension_semantics=("arbitrary", "arbitrary"),
                                             vmem_limit_bytes=VMEM_LIMIT),
        name="rg_scan",
    )(x, x, x, mod, g_pre, w_rec, conv_w, conv_b, w_gates, b_a, b_x, lam, h0)


def _rg_forward(x, mod, g_pre, uh, w_gates, b_a, b_x, lam, h0, h_rev, w_gate, w_out, g_post, *, tm):
    b, t, d = x.shape
    nt = t // tm
    main = pl.BlockSpec((1, tm, d), lambda bb, j: (bb, j, 0))
    carry_spec = pl.BlockSpec((1, SUBLANES, d), lambda bb, j: (bb, 0, 0))
    return pl.pallas_call(
        functools.partial(_rg_final_kernel, tm=tm, nt=nt),
        grid=(b, nt),
        in_specs=[main, _mod_spec(mod.shape[0]), _const_spec((1, d)), main, _const_spec(w_gates.shape),
                  _const_spec((1, d)), _const_spec((1, d)), _const_spec((1, d)), carry_spec, main,
                  _const_spec(w_gate.shape), _const_spec(w_out.shape), _const_spec((1, d))],
        out_specs=[main, carry_spec],
        out_shape=[jax.ShapeDtypeStruct(x.shape, F32), jax.ShapeDtypeStruct((b, SUBLANES, d), F32)],
        scratch_shapes=_scan_scratch(tm) + [pltpu.VMEM((tm, d), BF16), pltpu.VMEM((tm, d), F32),
                                            pltpu.VMEM((tm, d), BF16)],
        compiler_params=pltpu.CompilerParams(dimension_semantics=("arbitrary", "arbitrary"),
                                             vmem_limit_bytes=VMEM_LIMIT),
        name="rg_final",
    )(x, mod, g_pre, uh, w_gates, b_a, b_x, lam, h0, h_rev, w_gate, w_out, g_post)


def _pool_kernel(*refs, tm, nt, width, halo_rows):
    if halo_rows:
        (x_ref, xp_ref, xn_ref, mod_ref, gpre_ref, gpost_ref, band_ref, invcc_ref, rowinv_ref, pw_ref, pb_ref,
         ps_ref, o_ref, hext_scr, rs_scr) = refs
    else:
        (x_ref, mod_ref, gpre_ref, gpost_ref, band_ref, invcc_ref, rowinv_ref, pw_ref, pb_ref, ps_ref,
         o_ref, hext_scr, rs_scr) = refs
    j = pl.program_id(1)
    rt = tm // width
    shift, scale, gate = mod_ref[0, 0:1, :], mod_ref[0, 1:2, :], mod_ref[0, 2:3, :]
    gm = gpre_ref[...] * (1.0 + scale)

    def prep(v):
        return _norm_mod(v, gm, shift)

    base = halo_rows * width
    hext_scr[base:base + tm] = prep(x_ref[0])
    if halo_rows:
        hext_scr[0:base] = jnp.where(j > 0, prep(xp_ref[0]), 0.0)
        hext_scr[base + tm:base + tm + base] = jnp.where(j < nt - 1, prep(xn_ref[0]), 0.0)

    def row_body(r, _):
        for g, w in enumerate(POOL_WINDOWS):
            ln = slice(g * POOL_GROUP, (g + 1) * POOL_GROUP)
            offs = range(-(w // 2), w // 2) if halo_rows else range(0, 1)
            acc = None
            for dr in offs:
                start = pl.multiple_of((r + halo_rows + dr) * width, width)
                v = hext_scr[pl.ds(start, width), ln]
                acc = v if acc is None else acc + v
            inv = rowinv_ref[pl.ds(j * rt + r, 1), ln]
            rs_scr[pl.ds(pl.multiple_of(r * width, width), width), ln] = acc * inv
        return 0

    lax.fori_loop(0, rt, row_body, 0)

    for ch in range(tm // POOL_CHUNK):
        rows = slice(ch * POOL_CHUNK, (ch + 1) * POOL_CHUNK)
        for g in range(len(POOL_WINDOWS)):
            ln = slice(g * POOL_GROUP, (g + 1) * POOL_GROUP)
            rs = rs_scr[rows, ln]
            hi = rs.astype(BF16)
            lo = (rs - hi.astype(F32)).astype(BF16)
            pooled = (jnp.dot(band_ref[g], hi, preferred_element_type=F32)
                      + jnp.dot(band_ref[g], lo, preferred_element_type=F32)) * invcc_ref[:, ln]
            hg = hext_scr[base + ch * POOL_CHUNK:base + (ch + 1) * POOL_CHUNK, ln]
            yg = jnp.dot((pooled - hg).astype(BF16), pw_ref[g], preferred_element_type=F32)
            rs_scr[rows, ln] = (yg + pb_ref[:, ln]) * ps_ref[:, ln]

    o_ref[0] = x_ref[0] + _norm_mod(rs_scr[...], gate * gpost_ref[...], None)


def _pool_consts(n_rows, width):
    idx = np.arange(POOL_CHUNK)
    row, col = idx // width, idx % width
    band = np.zeros((len(POOL_WINDOWS), POOL_CHUNK, POOL_CHUNK), np.float32)
    invcc = np.zeros((POOL_CHUNK, D_MODEL), np.float32)
    rowinv = np.zeros((n_rows, D_MODEL), np.float32)
    r = np.arange(n_rows)
    for g, w in enumerate(POOL_WINDOWS):
        same = row[:, None] == row[None, :]
        d = col[None, :] - col[:, None]
        band[g] = (same & (d >= -(w // 2)) & (d < w // 2)).astype(np.float32)
        cc = np.minimum(col + w // 2, width) - np.maximum(col - w // 2, 0)
        invcc[:, g * POOL_GROUP:(g + 1) * POOL_GROUP] = (1.0 / cc)[:, None]
        cr = np.minimum(r + w // 2, n_rows) - np.maximum(r - w // 2, 0)
        rowinv[:, g * POOL_GROUP:(g + 1) * POOL_GROUP] = (1.0 / cr)[:, None]
    return jnp.asarray(band, BF16), jnp.asarray(invcc), jnp.asarray(rowinv)


def _pool(x, mod, g_pre, g_post, pw, pb, ps, *, width, tm):
    b, t, d = x.shape
    n_rows = t // width
    nt = t // tm
    halo_rows = max(POOL_WINDOWS) // 2 if nt > 1 else 0
    assert nt > 1 or n_rows == 1
    band, invcc, rowinv = _pool_consts(n_rows, width)
    main = pl.BlockSpec((1, tm, d), lambda bb, j: (bb, j, 0))
    in_specs, args = [main], [x]
    if halo_rows:
        hb = halo_rows * width
        per = tm // hb
        last = t // hb - 1
        in_specs += [pl.BlockSpec((1, hb, d), lambda bb, j: (bb, jnp.maximum(j * per - 1, 0), 0)),
                     pl.BlockSpec((1, hb, d), lambda bb, j: (bb, jnp.minimum((j + 1) * per, last), 0))]
        args += [x, x]
    in_specs += [_mod_spec(mod.shape[0]), _const_spec((1, d)), _const_spec((1, d)), _const_spec(band.shape),
                 _const_spec(invcc.shape), _const_spec(rowinv.shape), _const_spec(pw.shape),
                 _const_spec((1, d)), _const_spec((1, d))]
    args += [mod, g_pre, g_post, band, invcc, rowinv, pw, pb, ps]
    kern = functools.partial(_pool_kernel, tm=tm, nt=nt, width=width, halo_rows=halo_rows)
    return pl.pallas_call(
        kern,
        grid=(b, nt),
        in_specs=in_specs,
        out_specs=main,
        out_shape=jax.ShapeDtypeStruct(x.shape, F32),
        scratch_shapes=[pltpu.VMEM((tm + 2 * halo_rows * width, d), F32), pltpu.VMEM((tm, d), F32)],
        compiler_params=pltpu.CompilerParams(dimension_semantics=("parallel", "parallel"),
                                             vmem_limit_bytes=VMEM_LIMIT),
        name="pool_mix",
    )(*args)


def kernel(x, c, ctx, c_ctx, ada_w, ada_b, norm_pre_mix, norm_post_mix, norm_pre_ffn, norm_post_ffn, rg_w_in,
           rg_conv_w, rg_conv_b, rg_gate_a_w, rg_gate_a_b, rg_gate_x_w, rg_gate_x_b, rg_lambda, rg_w_out, pool_w,
           pool_b, pool_scale, ffn_w_up, ffn_conv_w, ffn_conv_b, ffn_w_down):
    bsz, seq, d = x.shape
    ctx_len = ctx.shape[1]
    depth = ada_w.shape[0]
    assert d == D_MODEL and seq % GRID_W == 0

    ct = jnp.zeros((d, SUBLANES), F32).at[:, :bsz].set(c.T).at[:, bsz].set(c_ctx)
    mods = _ada(ct, ada_w, ada_b, bsz + 1)
    row = lambda a, i: a[i].reshape(1, -1)
    tm_lat = 512
    zeros_state = jnp.zeros((bsz, SUBLANES, d), F32)

    for i in range(depth):
        last = i == depth - 1
        mod_l = mods[i, :bsz].reshape(bsz, N_MOD, d)
        mod_c = mods[i, bsz].reshape(1, N_MOD, d)
        g_pre, g_post = row(norm_pre_mix, i), row(norm_post_mix, i)
        jm = i // 2
        if i % 2 == 0:
            w_gate = rg_w_in[jm, :, :d].astype(BF16)
            w_rec = rg_w_in[jm, :, d:].astype(BF16)
            w_out = rg_w_out[jm].astype(BF16)
            conv_w, conv_b = rg_conv_w[jm], row(rg_conv_b, jm)
            dirs = []
            for dd in range(2):
                wg = jnp.concatenate([rg_gate_a_w[jm, dd], rg_gate_x_w[jm, dd]], axis=-1)
                zg = jnp.zeros_like(wg[0::2])
                wg = jnp.concatenate([jnp.concatenate([wg[0::2], zg], axis=-1),
                                      jnp.concatenate([zg, wg[1::2]], axis=-1)], axis=1).astype(BF16)
                dirs.append((wg, rg_gate_a_b[jm, dd].reshape(1, d), rg_gate_x_b[jm, dd].reshape(1, d),
                             rg_lambda[jm, dd].reshape(1, d)))
            hb_c, uh_c, end_b = _rg_reverse(ctx, mod_c, g_pre, w_rec, conv_w, conv_b, *dirs[1], zeros_state,
                                            tm=ctx_len)
            ctx_new, end_f = _rg_forward(ctx, mod_c, g_pre, uh_c, *dirs[0], zeros_state, hb_c, w_gate, w_out,
                                         g_post, tm=ctx_len)
            hb_l, uh_l, _ = _rg_reverse(x, mod_l, g_pre, w_rec, conv_w, conv_b, *dirs[1], end_b, tm=tm_lat)
            x, _ = _rg_forward(x, mod_l, g_pre, uh_l, *dirs[0], end_f, hb_l, w_gate, w_out, g_post, tm=tm_lat)
            if not last:
                ctx = ctx_new
        else:
            pw = pool_w[jm].astype(BF16)
            pb, ps = row(pool_b, jm), row(pool_scale, jm)
            x_new = _pool(x, mod_l, g_pre, g_post, pw, pb, ps, width=GRID_W, tm=1024)
            if not last:
                ctx = _pool(ctx, mod_c, g_pre, g_post, pw, pb, ps, width=ctx_len, tm=ctx_len)
            x = x_new

        g_pre, g_post = row(norm_pre_ffn, i), row(norm_post_ffn, i)
        w_up, w_dn = ffn_w_up[i].astype(BF16), ffn_w_down[i].astype(BF16)
        cw, cb = ffn_conv_w[i], row(ffn_conv_b, i)
        x = _ffn(x, mod_l, g_pre, g_post, w_up, cw, cb, w_dn, tm=tm_lat)
        if not last:
            ctx = _ffn(ctx, mod_c, g_pre, g_post, w_up, cw, cb, w_dn, tm=ctx_len)
    return x
```

```python
import functools
import math

import numpy as np
import jax
import jax.numpy as jnp
from jax import lax
from jax.experimental import pallas as pl
from jax.experimental.pallas import tpu as pltpu

D_MODEL = 1024
GRID_W = 64
N_MOD = 6
RG_HEADS = 8
RG_BLOCK = D_MODEL // RG_HEADS
RG_CONV = 4
RG_C = 8.0
POOL_WINDOWS = (2, 4, 8, 16)
POOL_GROUP = D_MODEL // len(POOL_WINDOWS)
D_FF = 3 * D_MODEL
EPS = 1e-6
TINY = 1e-30

SUBLANES = 8
LANES = 128
MXU_N = 512
HEADS_PER_TILE = MXU_N // RG_BLOCK
HALO = 8
POOL_CHUNK = 256
SCAN_SEGS = 16
VMEM_LIMIT = 60 * 1024 * 1024

F32 = jnp.float32
BF16 = jnp.bfloat16


def _norm_mod(v, gm, shift):
    r = lax.rsqrt(jnp.mean(v * v, axis=-1, keepdims=True) + EPS)
    y = (v * r) * gm
    return y if shift is None else y + shift


def _gelu(x):
    k1 = math.sqrt(2.0 / math.pi)
    hx = 0.5 * x
    return hx + hx * jnp.tanh(x * (k1 + (k1 * 0.044715) * (x * x)))


def _sigmoid(x):
    return 1.0 / (1.0 + jnp.exp(-x))


def _const_spec(shape):
    nd = len(shape)
    return pl.BlockSpec(shape, lambda *_: (0,) * nd, pipeline_mode=pl.Buffered(1))


def _tile_specs(tm, nt, reverse):
    per = tm // HALO
    last = nt * per - 1

    def tile(j):
        return (nt - 1 - j) if reverse else j

    main = pl.BlockSpec((1, tm, D_MODEL), lambda b, j: (b, tile(j), 0))
    prev = pl.BlockSpec((1, HALO, D_MODEL), lambda b, j: (b, jnp.maximum(tile(j) * per - 1, 0), 0))
    nxt = pl.BlockSpec((1, HALO, D_MODEL), lambda b, j: (b, jnp.minimum((tile(j) + 1) * per, last), 0))
    return main, prev, nxt


def _mod_spec(n_mod_rows):
    if n_mod_rows == 1:
        return pl.BlockSpec((1, N_MOD, D_MODEL), lambda b, j: (0, 0, 0))
    return pl.BlockSpec((1, N_MOD, D_MODEL), lambda b, j: (b, 0, 0))


def _ada_kernel(ct_ref, w_ref, b_ref, o_ref, *, n_rows):
    ct = ct_ref[...]
    st = ct * _sigmoid(ct)
    w = w_ref[0]
    rows = [jnp.sum(w * st[:, m:m + 1], axis=0, keepdims=True) for m in range(n_rows)]
    rows += [jnp.zeros_like(rows[0])] * (SUBLANES - n_rows)
    o_ref[0] = jnp.concatenate(rows, axis=0) + b_ref[0]


def _ada(ct, ada_w, ada_b, n_rows):
    depth, d, n = ada_w.shape
    tn = 1536
    return pl.pallas_call(
        functools.partial(_ada_kernel, n_rows=n_rows),
        grid=(depth, n // tn),
        in_specs=[pl.BlockSpec((d, SUBLANES), lambda i, k: (0, 0)),
                  pl.BlockSpec((1, d, tn), lambda i, k: (i, 0, k)),
                  pl.BlockSpec((1, 1, tn), lambda i, k: (i, 0, k))],
        out_specs=pl.BlockSpec((1, SUBLANES, tn), lambda i, k: (i, 0, k)),
        out_shape=jax.ShapeDtypeStruct((depth, SUBLANES, n), F32),
        compiler_params=pltpu.CompilerParams(dimension_semantics=("parallel", "parallel"),
                                             vmem_limit_bytes=VMEM_LIMIT),
        name="ada_mod",
    )(ct, ada_w, ada_b.reshape(depth, 1, n))


def _prep_ext(x_ref, xp_ref, xn_ref, g_pre, shift, scale, tile, nt):
    gm = g_pre * (1.0 + scale)

    def prep(v):
        return _norm_mod(v, gm, shift)

    hp = jnp.where(tile > 0, prep(xp_ref[0]), 0.0)
    hn = jnp.where(tile < nt - 1, prep(xn_ref[0]), 0.0)
    return jnp.concatenate([hp, prep(x_ref[0]), hn], axis=0)


def _shifted(slab_ref, slab, k, tm):
    if k == 0:
        return slab_ref[slab, pl.ds(HALO, tm), :]
    return slab_ref[slab, pl.ds(HALO + k, tm, stride=1), :]


def _ffn_kernel(x_ref, xp_ref, xn_ref, mod_ref, gpre_ref, gpost_ref, wup_ref, cw_ref, cb_ref, wdn_ref,
                o_ref, hext_ref, hm_ref, zs0_ref, zs1_ref, zv0_ref, zv1_ref, act_ref, *, tm, nt, fc):
    j = pl.program_id(1)
    shift, scale, gate = mod_ref[0, 3:4, :], mod_ref[0, 4:5, :], mod_ref[0, 5:6, :]
    hext = _prep_ext(x_ref, xp_ref, xn_ref, gpre_ref[...], shift, scale, j, nt)
    hext_ref[...] = hext.astype(BF16)
    hm_ref[...] = hext[HALO:HALO + tm].astype(BF16)
    ns = fc // LANES
    n_chunks = D_FF // fc
    zs_refs, zv_refs = (zs0_ref, zs1_ref), (zv0_ref, zv1_ref)

    def up(c):
        lo, zs_ref = c * fc, zs_refs[c % 2]
        zu = jnp.dot(hext_ref[...], wup_ref[:, lo:lo + fc], preferred_element_type=F32)
        for l in range(ns):
            zs_ref[l] = zu[:, l * LANES:(l + 1) * LANES]
        zv_refs[c % 2][...] = jnp.dot(hm_ref[...], wup_ref[:, D_FF + lo:D_FF + lo + fc],
                                      preferred_element_type=F32)

    def act(c):
        lo, zs_ref, zv_ref = c * fc, zs_refs[c % 2], zv_refs[c % 2]
        for l in range(ns):
            ln = slice(lo + l * LANES, lo + (l + 1) * LANES)
            cw = cw_ref[:, ln]
            u = (cw[0:1] * _shifted(zs_ref, l, -1, tm) + cw[1:2] * _shifted(zs_ref, l, 0, tm)
                 + cw[2:3] * _shifted(zs_ref, l, 1, tm) + cb_ref[:, ln])
            act_ref[:, ln] = (_gelu(u) * zv_ref[:, l * LANES:(l + 1) * LANES]).astype(BF16)

    up(0)
    for c in range(n_chunks):
        if c + 1 < n_chunks:
            up(c + 1)
        act(c)
    out = jnp.dot(act_ref[...], wdn_ref[...], preferred_element_type=F32)
    o_ref[0] = x_ref[0] + _norm_mod(out, gate * gpost_ref[...], None)


def _ffn(x, mod, g_pre, g_post, w_up, conv_w, conv_b, w_down, *, tm, fc=512):
    b, t, d = x.shape
    nt = t // tm
    main, prev, nxt = _tile_specs(tm, nt, False)
    kern = functools.partial(_ffn_kernel, tm=tm, nt=nt, fc=fc)
    return pl.pallas_call(
        kern,
        grid=(b, nt),
        in_specs=[main, prev, nxt, _mod_spec(mod.shape[0]),
                  _const_spec((1, d)), _const_spec((1, d)),
                  _const_spec(w_up.shape), _const_spec(conv_w.shape), _const_spec((1, D_FF)),
                  _const_spec(w_down.shape)],
        out_specs=pl.BlockSpec((1, tm, d), lambda bb, j: (bb, j, 0)),
        out_shape=jax.ShapeDtypeStruct(x.shape, F32),
        scratch_shapes=[pltpu.VMEM((tm + 2 * HALO, d), BF16), pltpu.VMEM((tm, d), BF16),
                        pltpu.VMEM((fc // LANES, tm + 2 * HALO, LANES), F32),
                        pltpu.VMEM((fc // LANES, tm + 2 * HALO, LANES), F32),
                        pltpu.VMEM((tm, fc), F32), pltpu.VMEM((tm, fc), F32), pltpu.VMEM((tm, D_FF), BF16)],
        compiler_params=pltpu.CompilerParams(dimension_semantics=("parallel", "parallel"),
                                             vmem_limit_bytes=VMEM_LIMIT),
        name="conv_ffn",
    )(x, x, x, mod, g_pre, g_post, w_up, conv_w, conv_b, w_down)


def _seg_pitch(seg):
    p = seg + 4
    assert p % 8 == 4, seg
    return p


def _scan_tile(a_scr, b_scr, h_scr, carry_ref, *, seg, reverse):
    pitch = _seg_pitch(seg)
    nh = a_scr.shape[0]
    nv = SCAN_SEGS // SUBLANES
    sub = lax.broadcasted_iota(jnp.int32, (SUBLANES, RG_BLOCK), 0)
    chains = [(hd, v) for hd in range(nh) for v in range(nv)]

    def rows(v, i):
        return pl.ds(v * SUBLANES * pitch + ((seg - 1 - i) if reverse else i), SUBLANES, stride=pitch)

    def pass1(i, st):
        out = []
        for k, (hd, v) in enumerate(chains):
            av = a_scr[hd, rows(v, i), :]
            out += [av * st[2 * k] + b_scr[hd, rows(v, i), :], st[2 * k + 1] * av]
        return tuple(out)

    init = tuple(jnp.zeros((SUBLANES, RG_BLOCK), F32) if k % 2 == 0 else jnp.ones((SUBLANES, RG_BLOCK), F32)
                 for k in range(2 * len(chains)))
    st = init
    for i in range(seg):
        st = pass1(i, st)

    first = SUBLANES - 1 if reverse else 0
    last = 0 if reverse else SUBLANES - 1
    cins = {}
    for hd in range(nh):
        carry = jnp.broadcast_to(carry_ref[hd], (SUBLANES, RG_BLOCK))
        for v in (reversed(range(nv)) if reverse else range(nv)):
            k = hd * nv + v
            e, pe = st[2 * k], st[2 * k + 1]
            cin = carry
            for _ in range(SUBLANES - 1):
                nxt = pltpu.roll(e + pe * cin, (SUBLANES - 1) if reverse else 1, axis=0)
                cin = jnp.where(sub == first, carry, nxt)
            cins[(hd, v)] = cin
            end = e + pe * cin
            carry = jnp.broadcast_to(end[last:last + 1, :], (SUBLANES, RG_BLOCK))
        carry_ref[hd] = carry[0:1, :]

    def pass2(i, hs):
        out = []
        for k, (hd, v) in enumerate(chains):
            h = a_scr[hd, rows(v, i), :] * hs[k] + b_scr[hd, rows(v, i), :]
            h_scr[hd, rows(v, i), :] = h
            out.append(h)
        return tuple(out)

    hs = tuple(cins[ch] for ch in chains)
    for i in range(seg):
        hs = pass2(i, hs)


def _rg_gates_scan(uh_of_head, after_gate_matmul, wg_ref, ba_ref, bx_ref, lam_ref, h0_ref, carry_out_ref,
                   a_scr, b_scr, h_scr, carry_ref, *, tm, nt, reverse):
    j = pl.program_id(1)
    seg = tm // SCAN_SEGS
    pitch = _seg_pitch(seg)

    @pl.when(j == 0)
    def _():
        for hd in range(RG_HEADS):
            carry_ref[hd] = h0_ref[0, 0:1, hd * RG_BLOCK:(hd + 1) * RG_BLOCK]

    lam = lam_ref[...]
    sp = jnp.maximum(-lam, 0.0) + jnp.log(1.0 + jnp.exp(-jnp.abs(lam)))
    nc = (-0.5 * RG_C * math.log2(math.e)) * sp
    half_ba, half_bx = 0.5 * ba_ref[...], 0.5 * bx_ref[...]
    for pair in range(RG_HEADS // 2):
        uhs = [uh_of_head(2 * pair + q) for q in range(2)]
        gz = jnp.dot(jnp.concatenate(uhs, axis=1).astype(BF16), wg_ref[pair], preferred_element_type=F32)
        after_gate_matmul(2 * pair + 1)
        for q in range(2):
            hd, uh = 2 * pair + q, uhs[q]
            ln = slice(hd * RG_BLOCK, (hd + 1) * RG_BLOCK)
            tr = jnp.tanh(gz[:, 2 * q * RG_BLOCK:(2 * q + 1) * RG_BLOCK] + half_ba[:, ln])
            ti = jnp.tanh(gz[:, (2 * q + 1) * RG_BLOCK:(2 * q + 2) * RG_BLOCK] + half_bx[:, ln])
            a = jnp.exp2(nc[:, ln] + nc[:, ln] * tr)
            y1 = 1.0 - a * a
            mult = y1 * lax.rsqrt(jnp.maximum(y1, TINY))
            bb = mult * (uh + uh * ti)
            for s in range(SCAN_SEGS):
                a_scr[hd, pl.ds(s * pitch, seg, stride=1), :] = a[s * seg:(s + 1) * seg]
                b_scr[hd, pl.ds(s * pitch, seg, stride=1), :] = bb[s * seg:(s + 1) * seg]

    _scan_tile(a_scr, b_scr, h_scr, carry_ref, seg=seg, reverse=reverse)

    @pl.when(j == nt - 1)
    def _():
        for hd in range(RG_HEADS):
            carry_out_ref[0, :, hd * RG_BLOCK:(hd + 1) * RG_BLOCK] = jnp.broadcast_to(
                carry_ref[hd], (SUBLANES, RG_BLOCK))


def _head_h(h_scr, hd, tm):
    seg = tm // SCAN_SEGS
    pitch = _seg_pitch(seg)
    return jnp.concatenate([h_scr[hd, pl.ds(s * pitch, seg, stride=1), :] for s in range(SCAN_SEGS)], axis=0)


def _rg_scan_kernel(x_ref, xp_ref, xn_ref, mod_ref, gpre_ref, wr_ref, cw_ref, cb_ref, wg_ref, ba_ref, bx_ref,
                    lam_ref, h0_ref, h_ref, uh_ref, carry_out_ref, hext_scr, rec0_scr, rec1_scr, a_scr, b_scr,
                    h_scr, carry_ref, *, tm, nt):
    tile = nt - 1 - pl.program_id(1)
    shift, scale = mod_ref[0, 0:1, :], mod_ref[0, 1:2, :]
    hext_scr[...] = _prep_ext(x_ref, xp_ref, xn_ref, gpre_ref[...], shift, scale, tile, nt).astype(BF16)
    half_cw, half_cb = 0.5 * cw_ref[...], 0.5 * cb_ref[...]

    rec_scrs = (rec0_scr, rec1_scr)
    assert len(rec_scrs) * HEADS_PER_TILE == RG_HEADS

    def rec_chunk(p):
        r = jnp.dot(hext_scr[...], wr_ref[:, p * MXU_N:(p + 1) * MXU_N], preferred_element_type=F32)
        for q in range(HEADS_PER_TILE):
            rec_scrs[p][q] = r[:, q * RG_BLOCK:(q + 1) * RG_BLOCK]

    rec_chunk(0)

    def after_gate_matmul(hd):
        p, q = divmod(hd, HEADS_PER_TILE)
        if q == 1 and p + 1 < len(rec_scrs):
            rec_chunk(p + 1)

    def uh_of_head(hd):
        p, q = divmod(hd, HEADS_PER_TILE)
        ln = slice(hd * RG_BLOCK, (hd + 1) * RG_BLOCK)
        uh = half_cb[:, ln] + sum(half_cw[k:k + 1, ln] * _shifted(rec_scrs[p], q, k - RG_CONV // 2, tm)
                                  for k in range(RG_CONV))
        uh_ref[0, :, ln] = uh
        return uh

    _rg_gates_scan(uh_of_head, after_gate_matmul, wg_ref, ba_ref, bx_ref, lam_ref, h0_ref, carry_out_ref,
                   a_scr, b_scr, h_scr, carry_ref, tm=tm, nt=nt, reverse=True)
    for hd in range(RG_HEADS):
        h_ref[0, :, hd * RG_BLOCK:(hd + 1) * RG_BLOCK] = _head_h(h_scr, hd, tm)


def _rg_final_kernel(x_ref, mod_ref, gpre_ref, uh_ref, wg_ref, ba_ref, bx_ref, lam_ref, h0_ref, hb_ref,
                     wgate_ref, wout_ref, gpost_ref, o_ref, carry_out_ref, a_scr, b_scr, h_scr, carry_ref,
                     hm_scr, gz_scr, gy_scr, *, tm, nt):
    shift, scale, gate = mod_ref[0, 0:1, :], mod_ref[0, 1:2, :], mod_ref[0, 2:3, :]
    hm_scr[...] = _norm_mod(x_ref[0], gpre_ref[...] * (1.0 + scale), shift).astype(BF16)

    def after_gate_matmul(hd):
        pass

    def uh_of_head(hd):
        if hd % 2 == 0:
            ln2 = slice(hd * RG_BLOCK, (hd + 2) * RG_BLOCK)
            gz_scr[:, ln2] = jnp.dot(hm_scr[...], wgate_ref[:, ln2], preferred_element_type=F32)
        return uh_ref[0, :, hd * RG_BLOCK:(hd + 1) * RG_BLOCK]

    _rg_gates_scan(uh_of_head, after_gate_matmul, wg_ref, ba_ref, bx_ref, lam_ref, h0_ref, carry_out_ref,
                   a_scr, b_scr, h_scr, carry_ref, tm=tm, nt=nt, reverse=False)
    for hd in range(RG_HEADS):
        ln = slice(hd * RG_BLOCK, (hd + 1) * RG_BLOCK)
        gy_scr[:, ln] = (_gelu(gz_scr[:, ln]) * (_head_h(h_scr, hd, tm) + hb_ref[0, :, ln])).astype(BF16)
    out = jnp.dot(gy_scr[...], wout_ref[...], preferred_element_type=F32)
    o_ref[0] = x_ref[0] + _norm_mod(out, gate * gpost_ref[...], None)


def _scan_scratch(tm):
    rows = SCAN_SEGS * _seg_pitch(tm // SCAN_SEGS)
    return [pltpu.VMEM((RG_HEADS, rows, RG_BLOCK), F32)] * 3 + [pltpu.VMEM((RG_HEADS, 1, RG_BLOCK), F32)]


def _rg_reverse(x, mod, g_pre, w_rec, conv_w, conv_b, w_gates, b_a, b_x, lam, h0, *, tm):
    b, t, d = x.shape
    nt = t // tm
    main, prev, nxt = _tile_specs(tm, nt, True)
    carry_spec = pl.BlockSpec((1, SUBLANES, d), lambda bb, j: (bb, 0, 0))
    return pl.pallas_call(
        functools.partial(_rg_scan_kernel, tm=tm, nt=nt),
        grid=(b, nt),
        in_specs=[main, prev, nxt, _mod_spec(mod.shape[0]), _const_spec((1, d)), _const_spec(w_rec.shape),
                  _const_spec(conv_w.shape), _const_spec((1, d)), _const_spec(w_gates.shape),
                  _const_spec((1, d)), _const_spec((1, d)), _const_spec((1, d)), carry_spec],
        out_specs=[main, main, carry_spec],
        out_shape=[jax.ShapeDtypeStruct(x.shape, F32), jax.ShapeDtypeStruct(x.shape, F32),
                   jax.ShapeDtypeStruct((b, SUBLANES, d), F32)],
        scratch_shapes=[pltpu.VMEM((tm + 2 * HALO, d), BF16),
                        pltpu.VMEM((HEADS_PER_TILE, tm + 2 * HALO, RG_BLOCK), F32),
                        pltpu.VMEM((HEADS_PER_TILE, tm + 2 * HALO, RG_BLOCK), F32)] + _scan_scratch(tm),
        compiler_params=pltpu.CompilerParams(dimension_semantics=("arbitrary", "arbitrary"),
                                             vmem_limit_bytes=VMEM_LIMIT),
        name="rg_scan",
    )(x, x, x, mod, g_pre, w_rec, conv_w, conv_b, w_gates, b_a, b_x, lam, h0)


def _rg_forward(x, mod, g_pre, uh, w_gates, b_a, b_x, lam, h0, h_rev, w_gate, w_out, g_post, *, tm):
    b, t, d = x.shape
    nt = t // tm
    main = pl.BlockSpec((1, tm, d), lambda bb, j: (bb, j, 0))
    carry_spec = pl.BlockSpec((1, SUBLANES, d), lambda bb, j: (bb, 0, 0))
    return pl.pallas_call(
        functools.partial(_rg_final_kernel, tm=tm, nt=nt),
        grid=(b, nt),
        in_specs=[main, _mod_spec(mod.shape[0]), _const_spec((1, d)), main, _const_spec(w_gates.shape),
                  _const_spec((1, d)), _const_spec((1, d)), _const_spec((1, d)), carry_spec, main,
                  _const_spec(w_gate.shape), _const_spec(w_out.shape), _const_spec((1, d))],
        out_specs=[main, carry_spec],
        out_shape=[jax.ShapeDtypeStruct(x.shape, F32), jax.ShapeDtypeStruct((b, SUBLANES, d), F32)],
        scratch_shapes=_scan_scratch(tm) + [pltpu.VMEM((tm, d), BF16), pltpu.VMEM((tm, d), F32),
                                            pltpu.VMEM((tm, d), BF16)],
        compiler_params=pltpu.CompilerParams(dimension_semantics=("arbitrary", "arbitrary"),
                                             vmem_limit_bytes=VMEM_LIMIT),
        name="rg_final",
    )(x, mod, g_pre, uh, w_gates, b_a, b_x, lam, h0, h_rev, w_gate, w_out, g_post)


def _pool_kernel(*refs, tm, nt, width, halo_rows):
    if halo_rows:
        (x_ref, xp_ref, xn_ref, mod_ref, gpre_ref, gpost_ref, band_ref, invcc_ref, rowinv_ref, pw_ref, pb_ref,
         ps_ref, o_ref, hext_scr, rs_scr) = refs
    else:
        (x_ref, mod_ref, gpre_ref, gpost_ref, band_ref, invcc_ref, rowinv_ref, pw_ref, pb_ref, ps_ref,
         o_ref, hext_scr, rs_scr) = refs
    j = pl.program_id(1)
    rt = tm // width
    shift, scale, gate = mod_ref[0, 0:1, :], mod_ref[0, 1:2, :], mod_ref[0, 2:3, :]
    gm = gpre_ref[...] * (1.0 + scale)

    def prep(v):
        return _norm_mod(v, gm, shift)

    base = halo_rows * width
    hext_scr[base:base + tm] = prep(x_ref[0])
    if halo_rows:
        hext_scr[0:base] = jnp.where(j > 0, prep(xp_ref[0]), 0.0)
        hext_scr[base + tm:base + tm + base] = jnp.where(j < nt - 1, prep(xn_ref[0]), 0.0)

    def grid_row(r, ln):
        return hext_scr[(r + halo_rows) * width:(r + halo_rows + 1) * width, ln]

    for g, w in enumerate(POOL_WINDOWS):
        ln = slice(g * POOL_GROUP, (g + 1) * POOL_GROUP)
        if halo_rows:
            win = grid_row(-(w // 2), ln)
            for dr in range(-(w // 2) + 1, w // 2):
                win = win + grid_row(dr, ln)
        else:
            win = grid_row(0, ln)
        for r in range(rt):
            rs_scr[r * width:(r + 1) * width, ln] = win * rowinv_ref[pl.ds(j * rt + r, 1), ln]
            if r + 1 < rt:
                win = win + grid_row(r + w // 2, ln) - grid_row(r - w // 2, ln)

    for ch in range(tm // POOL_CHUNK):
        rows = slice(ch * POOL_CHUNK, (ch + 1) * POOL_CHUNK)
        for g in range(len(POOL_WINDOWS)):
            ln = slice(g * POOL_GROUP, (g + 1) * POOL_GROUP)
            rs = rs_scr[rows, ln]
            hi = rs.astype(BF16)
            lo = (rs - hi.astype(F32)).astype(BF16)
            pooled = (jnp.dot(band_ref[g], hi, preferred_element_type=F32)
                      + jnp.dot(band_ref[g], lo, preferred_element_type=F32)) * invcc_ref[:, ln]
            hg = hext_scr[base + ch * POOL_CHUNK:base + (ch + 1) * POOL_CHUNK, ln]
            yg = jnp.dot((pooled - hg).astype(BF16), pw_ref[g], preferred_element_type=F32)
            rs_scr[rows, ln] = (yg + pb_ref[:, ln]) * ps_ref[:, ln]

    o_ref[0] = x_ref[0] + _norm_mod(rs_scr[...], gate * gpost_ref[...], None)


def _pool_consts(n_rows, width):
    idx = np.arange(POOL_CHUNK)
    row, col = idx // width, idx % width
    band = np.zeros((len(POOL_WINDOWS), POOL_CHUNK, POOL_CHUNK), np.float32)
    invcc = np.zeros((POOL_CHUNK, D_MODEL), np.float32)
    rowinv = np.zeros((n_rows, D_MODEL), np.float32)
    r = np.arange(n_rows)
    for g, w in enumerate(POOL_WINDOWS):
        same = row[:, None] == row[None, :]
        d = col[None, :] - col[:, None]
        band[g] = (same & (d >= -(w // 2)) & (d < w // 2)).astype(np.float32)
        cc = np.minimum(col + w // 2, width) - np.maximum(col - w // 2, 0)
        invcc[:, g * POOL_GROUP:(g + 1) * POOL_GROUP] = (1.0 / cc)[:, None]
        cr = np.minimum(r + w // 2, n_rows) - np.maximum(r - w // 2, 0)
        rowinv[:, g * POOL_GROUP:(g + 1) * POOL_GROUP] = (1.0 / cr)[:, None]
    return jnp.asarray(band, BF16), jnp.asarray(invcc), jnp.asarray(rowinv)


def _pool(x, mod, g_pre, g_post, pw, pb, ps, *, width, tm):
    b, t, d = x.shape
    n_rows = t // width
    nt = t // tm
    halo_rows = max(POOL_WINDOWS) // 2 if nt > 1 else 0
    assert nt > 1 or n_rows == 1
    band, invcc, rowinv = _pool_consts(n_rows, width)
    main = pl.BlockSpec((1, tm, d), lambda bb, j: (bb, j, 0))
    in_specs, args = [main], [x]
    if halo_rows:
        hb = halo_rows * width
        per = tm // hb
        last = t // hb - 1
        in_specs += [pl.BlockSpec((1, hb, d), lambda bb, j: (bb, jnp.maximum(j * per - 1, 0), 0)),
                     pl.BlockSpec((1, hb, d), lambda bb, j: (bb, jnp.minimum((j + 1) * per, last), 0))]
        args += [x, x]
    in_specs += [_mod_spec(mod.shape[0]), _const_spec((1, d)), _const_spec((1, d)), _const_spec(band.shape),
                 _const_spec(invcc.shape), _const_spec(rowinv.shape), _const_spec(pw.shape),
                 _const_spec((1, d)), _const_spec((1, d))]
    args += [mod, g_pre, g_post, band, invcc, rowinv, pw, pb, ps]
    kern = functools.partial(_pool_kernel, tm=tm, nt=nt, width=width, halo_rows=halo_rows)
    return pl.pallas_call(
        kern,
        grid=(b, nt),
        in_specs=in_specs,
        out_specs=main,
        out_shape=jax.ShapeDtypeStruct(x.shape, F32),
        scratch_shapes=[pltpu.VMEM((tm + 2 * halo_rows * width, d), F32), pltpu.VMEM((tm, d), F32)],
        compiler_params=pltpu.CompilerParams(dimension_semantics=("parallel", "parallel"),
                                             vmem_limit_bytes=VMEM_LIMIT),
        name="pool_mix",
    )(*args)


def kernel(x, c, ctx, c_ctx, ada_w, ada_b, norm_pre_mix, norm_post_mix, norm_pre_ffn, norm_post_ffn, rg_w_in,
           rg_conv_w, rg_conv_b, rg_gate_a_w, rg_gate_a_b, rg_gate_x_w, rg_gate_x_b, rg_lambda, rg_w_out, pool_w,
           pool_b, pool_scale, ffn_w_up, ffn_conv_w, ffn_conv_b, ffn_w_down):
    bsz, seq, d = x.shape
    ctx_len = ctx.shape[1]
    depth = ada_w.shape[0]
    assert d == D_MODEL and seq % GRID_W == 0

    ct = jnp.zeros((d, SUBLANES), F32).at[:, :bsz].set(c.T).at[:, bsz].set(c_ctx)
    mods = _ada(ct, ada_w, ada_b, bsz + 1)
    row = lambda a, i: a[i].reshape(1, -1)
    tm_lat = 512
    zeros_state = jnp.zeros((bsz, SUBLANES, d), F32)

    for i in range(depth):
        last = i == depth - 1
        mod_l = mods[i, :bsz].reshape(bsz, N_MOD, d)
        mod_c = mods[i, bsz].reshape(1, N_MOD, d)
        g_pre, g_post = row(norm_pre_mix, i), row(norm_post_mix, i)
        jm = i // 2
        if i % 2 == 0:
            w_gate = rg_w_in[jm, :, :d].astype(BF16)
            w_rec = rg_w_in[jm, :, d:].astype(BF16)
            w_out = rg_w_out[jm].astype(BF16)
            conv_w, conv_b = rg_conv_w[jm], row(rg_conv_b, jm)
            dirs = []
            for dd in range(2):
                wg = jnp.concatenate([rg_gate_a_w[jm, dd], rg_gate_x_w[jm, dd]], axis=-1)
                zg = jnp.zeros_like(wg[0::2])
                wg = jnp.concatenate([jnp.concatenate([wg[0::2], zg], axis=-1),
                                      jnp.concatenate([zg, wg[1::2]], axis=-1)], axis=1).astype(BF16)
                dirs.append((wg, rg_gate_a_b[jm, dd].reshape(1, d), rg_gate_x_b[jm, dd].reshape(1, d),
                             rg_lambda[jm, dd].reshape(1, d)))
            hb_c, uh_c, end_b = _rg_reverse(ctx, mod_c, g_pre, w_rec, conv_w, conv_b, *dirs[1], zeros_state,
                                            tm=ctx_len)
            ctx_new, end_f = _rg_forward(ctx, mod_c, g_pre, uh_c, *dirs[0], zeros_state, hb_c, w_gate, w_out,
                                         g_post, tm=ctx_len)
            hb_l, uh_l, _ = _rg_reverse(x, mod_l, g_pre, w_rec, conv_w, conv_b, *dirs[1], end_b, tm=tm_lat)
            x, _ = _rg_forward(x, mod_l, g_pre, uh_l, *dirs[0], end_f, hb_l, w_gate, w_out, g_post, tm=tm_lat)
            if not last:
                ctx = ctx_new
        else:
            pw = pool_w[jm].astype(BF16)
            pb, ps = row(pool_b, jm), row(pool_scale, jm)
            x_new = _pool(x, mod_l, g_pre, g_post, pw, pb, ps, width=GRID_W, tm=1024)
            if not last:
                ctx = _pool(ctx, mod_c, g_pre, g_post, pw, pb, ps, width=ctx_len, tm=ctx_len)
            x = x_new

        g_pre, g_post = row(norm_pre_ffn, i), row(norm_post_ffn, i)
        w_up, w_dn = ffn_w_up[i].astype(BF16), ffn_w_down[i].astype(BF16)
        cw, cb = ffn_conv_w[i], row(ffn_conv_b, i)
        x = _ffn(x, mod_l, g_pre, g_post, w_up, cw, cb, w_dn, tm=1024)
        if not last:
            ctx = _ffn(ctx, mod_c, g_pre, g_post, w_up, cw, cb, w_dn, tm=ctx_len)
    return x
```

```python
import functools
import math

import numpy as np
import jax
import jax.numpy as jnp
from jax import lax
from jax.experimental import pallas as pl
from jax.experimental.pallas import tpu as pltpu

D_MODEL = 1024
GRID_W = 64
N_MOD = 6
RG_HEADS = 8
RG_BLOCK = D_MODEL // RG_HEADS
RG_CONV = 4
RG_C = 8.0
POOL_WINDOWS = (2, 4, 8, 16)
POOL_GROUP = D_MODEL // len(POOL_WINDOWS)
D_FF = 3 * D_MODEL
EPS = 1e-6
TINY = 1e-30

SUBLANES = 8
LANES = 128
BF16_ROWS = 16
MXU_N = 512
HEADS_PER_TILE = MXU_N // RG_BLOCK
HALO = 8
POOL_CHUNK = 256
SCAN_SEGS = 16
VMEM_LIMIT = 60 * 1024 * 1024

F32 = jnp.float32
BF16 = jnp.bfloat16


def _norm_mod(v, gm, shift):
    r = lax.rsqrt(jnp.mean(v * v, axis=-1, keepdims=True) + EPS)
    y = (v * r) * gm
    return y if shift is None else y + shift


def _gelu(x):
    k1 = math.sqrt(2.0 / math.pi)
    hx = 0.5 * x
    return hx + hx * jnp.tanh(x * (k1 + (k1 * 0.044715) * (x * x)))


def _sigmoid(x):
    return 1.0 / (1.0 + jnp.exp(-x))


def _const_spec(shape):
    nd = len(shape)
    return pl.BlockSpec(shape, lambda *_: (0,) * nd, pipeline_mode=pl.Buffered(1))


def _tile_specs(tm, nt, reverse):
    per = tm // HALO
    last = nt * per - 1

    def tile(j):
        return (nt - 1 - j) if reverse else j

    main = pl.BlockSpec((1, tm, D_MODEL), lambda b, j: (b, tile(j), 0))
    prev = pl.BlockSpec((1, HALO, D_MODEL), lambda b, j: (b, jnp.maximum(tile(j) * per - 1, 0), 0))
    nxt = pl.BlockSpec((1, HALO, D_MODEL), lambda b, j: (b, jnp.minimum((tile(j) + 1) * per, last), 0))
    return main, prev, nxt


def _mod_spec(n_mod_rows):
    if n_mod_rows == 1:
        return pl.BlockSpec((1, N_MOD, D_MODEL), lambda b, j: (0, 0, 0))
    return pl.BlockSpec((1, N_MOD, D_MODEL), lambda b, j: (b, 0, 0))


def _ada_kernel(ct_ref, w_ref, b_ref, o_ref, *, n_rows):
    ct = ct_ref[...]
    st = ct * _sigmoid(ct)
    w = w_ref[0]
    rows = [jnp.sum(w * st[:, m:m + 1], axis=0, keepdims=True) for m in range(n_rows)]
    rows += [jnp.zeros_like(rows[0])] * (SUBLANES - n_rows)
    o_ref[0] = jnp.concatenate(rows, axis=0) + b_ref[0]


def _ada(ct, ada_w, ada_b, n_rows):
    depth, d, n = ada_w.shape
    tn = 1536
    return pl.pallas_call(
        functools.partial(_ada_kernel, n_rows=n_rows),
        grid=(depth, n // tn),
        in_specs=[pl.BlockSpec((d, SUBLANES), lambda i, k: (0, 0)),
                  pl.BlockSpec((1, d, tn), lambda i, k: (i, 0, k)),
                  pl.BlockSpec((1, 1, tn), lambda i, k: (i, 0, k))],
        out_specs=pl.BlockSpec((1, SUBLANES, tn), lambda i, k: (i, 0, k)),
        out_shape=jax.ShapeDtypeStruct((depth, SUBLANES, n), F32),
        compiler_params=pltpu.CompilerParams(dimension_semantics=("parallel", "parallel"),
                                             vmem_limit_bytes=VMEM_LIMIT),
        name="ada_mod",
    )(ct, ada_w, ada_b.reshape(depth, 1, n))


def _prep_ext(x_ref, xp_ref, xn_ref, g_pre, shift, scale, tile, nt):
    gm = g_pre * (1.0 + scale)

    def prep(v):
        return _norm_mod(v, gm, shift)

    hp = jnp.where(tile > 0, prep(xp_ref[0]), 0.0)
    hn = jnp.where(tile < nt - 1, prep(xn_ref[0]), 0.0)
    return jnp.concatenate([hp, prep(x_ref[0]), hn], axis=0)


def _shifted(slab_ref, slab, k, tm):
    if k == 0:
        return slab_ref[slab, pl.ds(HALO, tm), :]
    return slab_ref[slab, pl.ds(HALO + k, tm, stride=1), :]


def _ffn_kernel(x_ref, xp_ref, xn_ref, mod_ref, gpre_ref, gpost_ref, wup_ref, cw_ref, cb_ref, wdn_ref,
                o_ref, hext_ref, hm_ref, zs0_ref, zs1_ref, zv0_ref, zv1_ref, act_ref, *, tm, nt, fc):
    j = pl.program_id(1)
    shift, scale, gate = mod_ref[0, 3:4, :], mod_ref[0, 4:5, :], mod_ref[0, 5:6, :]
    hext = _prep_ext(x_ref, xp_ref, xn_ref, gpre_ref[...], shift, scale, j, nt)
    hext_ref[...] = hext.astype(BF16)
    hm_ref[...] = hext[HALO:HALO + tm].astype(BF16)
    ns = fc // LANES
    n_chunks = D_FF // fc
    zs_refs, zv_refs = (zs0_ref, zs1_ref), (zv0_ref, zv1_ref)

    def up(c):
        lo, zs_ref = c * fc, zs_refs[c % 2]
        zu = jnp.dot(hext_ref[...], wup_ref[:, lo:lo + fc], preferred_element_type=F32)
        for l in range(ns):
            zs_ref[l] = zu[:, l * LANES:(l + 1) * LANES]
        zv_refs[c % 2][...] = jnp.dot(hm_ref[...], wup_ref[:, D_FF + lo:D_FF + lo + fc],
                                      preferred_element_type=F32)

    def act(c):
        lo, zs_ref, zv_ref = c * fc, zs_refs[c % 2], zv_refs[c % 2]
        for l in range(ns):
            ln = slice(lo + l * LANES, lo + (l + 1) * LANES)
            cw = cw_ref[:, ln]
            u = (cw[0:1] * _shifted(zs_ref, l, -1, tm) + cw[1:2] * _shifted(zs_ref, l, 0, tm)
                 + cw[2:3] * _shifted(zs_ref, l, 1, tm) + cb_ref[:, ln])
            act_ref[:, ln] = (_gelu(u) * zv_ref[:, l * LANES:(l + 1) * LANES]).astype(BF16)

    up(0)
    for c in range(n_chunks):
        if c + 1 < n_chunks:
            up(c + 1)
        act(c)
    out = jnp.dot(act_ref[...], wdn_ref[...], preferred_element_type=F32)
    o_ref[0] = x_ref[0] + _norm_mod(out, gate * gpost_ref[...], None)


def _ffn(x, mod, g_pre, g_post, w_up, conv_w, conv_b, w_down, *, tm, fc=512):
    b, t, d = x.shape
    nt = t // tm
    main, prev, nxt = _tile_specs(tm, nt, False)
    kern = functools.partial(_ffn_kernel, tm=tm, nt=nt, fc=fc)
    return pl.pallas_call(
        kern,
        grid=(b, nt),
        in_specs=[main, prev, nxt, _mod_spec(mod.shape[0]),
                  _const_spec((1, d)), _const_spec((1, d)),
                  _const_spec(w_up.shape), _const_spec(conv_w.shape), _const_spec((1, D_FF)),
                  _const_spec(w_down.shape)],
        out_specs=pl.BlockSpec((1, tm, d), lambda bb, j: (bb, j, 0)),
        out_shape=jax.ShapeDtypeStruct(x.shape, F32),
        scratch_shapes=[pltpu.VMEM((tm + 2 * HALO, d), BF16), pltpu.VMEM((tm, d), BF16),
                        pltpu.VMEM((fc // LANES, tm + 2 * HALO, LANES), F32),
                        pltpu.VMEM((fc // LANES, tm + 2 * HALO, LANES), F32),
                        pltpu.VMEM((tm, fc), F32), pltpu.VMEM((tm, fc), F32), pltpu.VMEM((tm, D_FF), BF16)],
        compiler_params=pltpu.CompilerParams(dimension_semantics=("parallel", "parallel"),
                                             vmem_limit_bytes=VMEM_LIMIT),
        name="conv_ffn",
    )(x, x, x, mod, g_pre, g_post, w_up, conv_w, conv_b, w_down)


def _seg_pitch(seg):
    p = seg + 4
    assert p % 8 == 4, seg
    return p


def _scan_tile(a_scr, b_scr, h_scr, carry_ref, *, seg, reverse):
    pitch = _seg_pitch(seg)
    nh = a_scr.shape[0]
    nv = SCAN_SEGS // SUBLANES
    sub = lax.broadcasted_iota(jnp.int32, (SUBLANES, RG_BLOCK), 0)
    chains = [(hd, v) for hd in range(nh) for v in range(nv)]

    def rows(v, i):
        return pl.ds(v * SUBLANES * pitch + ((seg - 1 - i) if reverse else i), SUBLANES, stride=pitch)

    def pass1(i, st):
        out = []
        for k, (hd, v) in enumerate(chains):
            av = a_scr[hd, rows(v, i), :]
            out += [av * st[2 * k] + b_scr[hd, rows(v, i), :], st[2 * k + 1] * av]
        return tuple(out)

    init = tuple(jnp.zeros((SUBLANES, RG_BLOCK), F32) if k % 2 == 0 else jnp.ones((SUBLANES, RG_BLOCK), F32)
                 for k in range(2 * len(chains)))
    st = init
    for i in range(seg):
        st = pass1(i, st)

    first = SUBLANES - 1 if reverse else 0
    last = 0 if reverse else SUBLANES - 1
    cins = {}
    for hd in range(nh):
        carry = jnp.broadcast_to(carry_ref[hd], (SUBLANES, RG_BLOCK))
        for v in (reversed(range(nv)) if reverse else range(nv)):
            k = hd * nv + v
            e, pe = st[2 * k], st[2 * k + 1]
            cin = carry
            for _ in range(SUBLANES - 1):
                nxt = pltpu.roll(e + pe * cin, (SUBLANES - 1) if reverse else 1, axis=0)
                cin = jnp.where(sub == first, carry, nxt)
            cins[(hd, v)] = cin
            end = e + pe * cin
            carry = jnp.broadcast_to(end[last:last + 1, :], (SUBLANES, RG_BLOCK))
        carry_ref[hd] = carry[0:1, :]

    def pass2(i, hs):
        out = []
        for k, (hd, v) in enumerate(chains):
            h = a_scr[hd, rows(v, i), :] * hs[k] + b_scr[hd, rows(v, i), :]
            h_scr[hd, rows(v, i), :] = h
            out.append(h)
        return tuple(out)

    hs = tuple(cins[ch] for ch in chains)
    for i in range(seg):
        hs = pass2(i, hs)


def _rg_gates_scan(uh_of_head, after_gate_matmul, wg_ref, ba_ref, bx_ref, lam_ref, h0_ref, carry_out_ref,
                   a_scr, b_scr, h_scr, carry_ref, *, tm, nt, reverse):
    j = pl.program_id(1)
    seg = tm // SCAN_SEGS
    pitch = _seg_pitch(seg)

    @pl.when(j == 0)
    def _():
        for hd in range(RG_HEADS):
            carry_ref[hd] = h0_ref[0, 0:1, hd * RG_BLOCK:(hd + 1) * RG_BLOCK]

    lam = lam_ref[...]
    sp = jnp.maximum(-lam, 0.0) + jnp.log(1.0 + jnp.exp(-jnp.abs(lam)))
    nc = (-0.5 * RG_C * math.log2(math.e)) * sp
    half_ba, half_bx = 0.5 * ba_ref[...], 0.5 * bx_ref[...]
    def gate_matmul(pair):
        uhs = [uh_of_head(2 * pair + q) for q in range(2)]
        gz = jnp.dot(jnp.concatenate(uhs, axis=1).astype(BF16), wg_ref[pair], preferred_element_type=F32)
        after_gate_matmul(2 * pair + 1)
        return uhs, gz

    n_pairs = RG_HEADS // 2
    queued = gate_matmul(0)
    for pair in range(n_pairs):
        uhs, gz = queued
        if pair + 1 < n_pairs:
            queued = gate_matmul(pair + 1)
        for q in range(2):
            hd, uh = 2 * pair + q, uhs[q]
            ln = slice(hd * RG_BLOCK, (hd + 1) * RG_BLOCK)
            tr = jnp.tanh(gz[:, 2 * q * RG_BLOCK:(2 * q + 1) * RG_BLOCK] + half_ba[:, ln])
            ti = jnp.tanh(gz[:, (2 * q + 1) * RG_BLOCK:(2 * q + 2) * RG_BLOCK] + half_bx[:, ln])
            a = jnp.exp2(nc[:, ln] + nc[:, ln] * tr)
            y1 = 1.0 - a * a
            mult = y1 * lax.rsqrt(jnp.maximum(y1, TINY))
            bb = mult * (uh + uh * ti)
            for s in range(SCAN_SEGS):
                a_scr[hd, pl.ds(s * pitch, seg, stride=1), :] = a[s * seg:(s + 1) * seg]
                b_scr[hd, pl.ds(s * pitch, seg, stride=1), :] = bb[s * seg:(s + 1) * seg]

    _scan_tile(a_scr, b_scr, h_scr, carry_ref, seg=seg, reverse=reverse)

    @pl.when(j == nt - 1)
    def _():
        for hd in range(RG_HEADS):
            carry_out_ref[0, :, hd * RG_BLOCK:(hd + 1) * RG_BLOCK] = jnp.broadcast_to(
                carry_ref[hd], (SUBLANES, RG_BLOCK))


def _head_h(h_scr, hd, tm):
    seg = tm // SCAN_SEGS
    pitch = _seg_pitch(seg)
    return jnp.concatenate([h_scr[hd, pl.ds(s * pitch, seg, stride=1), :] for s in range(SCAN_SEGS)], axis=0)


def _rg_scan_kernel(x_ref, xp_ref, xn_ref, mod_ref, gpre_ref, wr_ref, cw_ref, cb_ref, wg_ref, ba_ref, bx_ref,
                    lam_ref, h0_ref, h_ref, uh_ref, carry_out_ref, hext_scr, rec0_scr, rec1_scr, a_scr, b_scr,
                    h_scr, carry_ref, *, tm, nt):
    tile = nt - 1 - pl.program_id(1)
    shift, scale = mod_ref[0, 0:1, :], mod_ref[0, 1:2, :]
    gm = gpre_ref[...] * (1.0 + scale)
    half_cw, half_cb = 0.5 * cw_ref[...], 0.5 * cb_ref[...]

    rec_scrs = (rec0_scr, rec1_scr)
    assert len(rec_scrs) * HEADS_PER_TILE == RG_HEADS
    n_ext = tm + 2 * HALO
    rows_a = -(-(HALO + tm // 2) // BF16_ROWS) * BF16_ROWS

    def rec_chunk(p, lo, hi):
        r = jnp.dot(hext_scr[lo:hi], wr_ref[:, p * MXU_N:(p + 1) * MXU_N], preferred_element_type=F32)
        for q in range(HEADS_PER_TILE):
            rec_scrs[p][q, lo:hi] = r[:, q * RG_BLOCK:(q + 1) * RG_BLOCK]

    hp = jnp.where(tile > 0, _norm_mod(xp_ref[0], gm, shift), 0.0)
    hext_scr[:rows_a] = jnp.concatenate([hp, _norm_mod(x_ref[0, :rows_a - HALO], gm, shift)],
                                        axis=0).astype(BF16)
    rec_chunk(0, 0, rows_a)
    hn = jnp.where(tile < nt - 1, _norm_mod(xn_ref[0], gm, shift), 0.0)
    hext_scr[rows_a:] = jnp.concatenate([_norm_mod(x_ref[0, rows_a - HALO:], gm, shift), hn],
                                        axis=0).astype(BF16)
    rec_chunk(0, rows_a, n_ext)

    def after_gate_matmul(hd):
        p, q = divmod(hd, HEADS_PER_TILE)
        if q == HEADS_PER_TILE - 1 and p + 1 < len(rec_scrs):
            rec_chunk(p + 1, 0, n_ext)

    def uh_of_head(hd):
        p, q = divmod(hd, HEADS_PER_TILE)
        ln = slice(hd * RG_BLOCK, (hd + 1) * RG_BLOCK)
        uh = half_cb[:, ln] + sum(half_cw[k:k + 1, ln] * _shifted(rec_scrs[p], q, k - RG_CONV // 2, tm)
                                  for k in range(RG_CONV))
        uh_ref[0, :, ln] = uh
        return uh

    _rg_gates_scan(uh_of_head, after_gate_matmul, wg_ref, ba_ref, bx_ref, lam_ref, h0_ref, carry_out_ref,
                   a_scr, b_scr, h_scr, carry_ref, tm=tm, nt=nt, reverse=True)
    for hd in range(RG_HEADS):
        h_ref[0, :, hd * RG_BLOCK:(hd + 1) * RG_BLOCK] = _head_h(h_scr, hd, tm)


def _rg_final_kernel(x_ref, mod_ref, gpre_ref, uh_ref, wg_ref, ba_ref, bx_ref, lam_ref, h0_ref, hb_ref,
                     wgate_ref, wout_ref, gpost_ref, o_ref, carry_out_ref, a_scr, b_scr, h_scr, carry_ref,
                     hm_scr, gz_scr, gy_scr, *, tm, nt):
    shift, scale, gate = mod_ref[0, 0:1, :], mod_ref[0, 1:2, :], mod_ref[0, 2:3, :]
    hm_scr[...] = _norm_mod(x_ref[0], gpre_ref[...] * (1.0 + scale), shift).astype(BF16)

    def after_gate_matmul(hd):
        pass

    def uh_of_head(hd):
        if hd % 2 == 0:
            ln2 = slice(hd * RG_BLOCK, (hd + 2) * RG_BLOCK)
            gz_scr[:, ln2] = jnp.dot(hm_scr[...], wgate_ref[:, ln2], preferred_element_type=F32)
        return uh_ref[0, :, hd * RG_BLOCK:(hd + 1) * RG_BLOCK]

    _rg_gates_scan(uh_of_head, after_gate_matmul, wg_ref, ba_ref, bx_ref, lam_ref, h0_ref, carry_out_ref,
                   a_scr, b_scr, h_scr, carry_ref, tm=tm, nt=nt, reverse=False)
    for hd in range(RG_HEADS):
        ln = slice(hd * RG_BLOCK, (hd + 1) * RG_BLOCK)
        gy_scr[:, ln] = (_gelu(gz_scr[:, ln]) * (_head_h(h_scr, hd, tm) + hb_ref[0, :, ln])).astype(BF16)
    out = jnp.dot(gy_scr[...], wout_ref[...], preferred_element_type=F32)
    o_ref[0] = x_ref[0] + _norm_mod(out, gate * gpost_ref[...], None)


def _scan_scratch(tm):
    rows = SCAN_SEGS * _seg_pitch(tm // SCAN_SEGS)
    return [pltpu.VMEM((RG_HEADS, rows, RG_BLOCK), F32)] * 3 + [pltpu.VMEM((RG_HEADS, 1, RG_BLOCK), F32)]


def _rg_reverse(x, mod, g_pre, w_rec, conv_w, conv_b, w_gates, b_a, b_x, lam, h0, *, tm):
    b, t, d = x.shape
    nt = t // tm
    main, prev, nxt = _tile_specs(tm, nt, True)
    carry_spec = pl.BlockSpec((1, SUBLANES, d), lambda bb, j: (bb, 0, 0))
    return pl.pallas_call(
        functools.partial(_rg_scan_kernel, tm=tm, nt=nt),
        grid=(b, nt),
        in_specs=[main, prev, nxt, _mod_spec(mod.shape[0]), _const_spec((1, d)), _const_spec(w_rec.shape),
                  _const_spec(conv_w.shape), _const_spec((1, d)), _const_spec(w_gates.shape),
                  _const_spec((1, d)), _const_spec((1, d)), _const_spec((1, d)), carry_spec],
        out_specs=[main, main, carry_spec],
        out_shape=[jax.ShapeDtypeStruct(x.shape, F32), jax.ShapeDtypeStruct(x.shape, F32),
                   jax.ShapeDtypeStruct((b, SUBLANES, d), F32)],
        scratch_shapes=[pltpu.VMEM((tm + 2 * HALO, d), BF16),
                        pltpu.VMEM((HEADS_PER_TILE, tm + 2 * HALO, RG_BLOCK), F32),
                        pltpu.VMEM((HEADS_PER_TILE, tm + 2 * HALO, RG_BLOCK), F32)] + _scan_scratch(tm),
        compiler_params=pltpu.CompilerParams(dimension_semantics=("arbitrary", "arbitrary"),
                                             vmem_limit_bytes=VMEM_LIMIT),
        name="rg_scan",
    )(x, x, x, mod, g_pre, w_rec, conv_w, conv_b, w_gates, b_a, b_x, lam, h0)


def _rg_forward(x, mod, g_pre, uh, w_gates, b_a, b_x, lam, h0, h_rev, w_gate, w_out, g_post, *, tm):
    b, t, d = x.shape
    nt = t // tm
    main = pl.BlockSpec((1, tm, d), lambda bb, j: (bb, j, 0))
    carry_spec = pl.BlockSpec((1, SUBLANES, d), lambda bb, j: (bb, 0, 0))
    return pl.pallas_call(
        functools.partial(_rg_final_kernel, tm=tm, nt=nt),
        grid=(b, nt),
        in_specs=[main, _mod_spec(mod.shape[0]), _const_spec((1, d)), main, _const_spec(w_gates.shape),
                  _const_spec((1, d)), _const_spec((1, d)), _const_spec((1, d)), carry_spec, main,
                  _const_spec(w_gate.shape), _const_spec(w_out.shape), _const_spec((1, d))],
        out_specs=[main, carry_spec],
        out_shape=[jax.ShapeDtypeStruct(x.shape, F32), jax.ShapeDtypeStruct((b, SUBLANES, d), F32)],
        scratch_shapes=_scan_scratch(tm) + [pltpu.VMEM((tm, d), BF16), pltpu.VMEM((tm, d), F32),
                                            pltpu.VMEM((tm, d), BF16)],
        compiler_params=pltpu.CompilerParams(dimension_semantics=("arbitrary", "arbitrary"),
                                             vmem_limit_bytes=VMEM_LIMIT),
        name="rg_final",
    )(x, mod, g_pre, uh, w_gates, b_a, b_x, lam, h0, h_rev, w_gate, w_out, g_post)


def _pool_kernel(*refs, tm, nt, width, halo_rows):
    if halo_rows:
        (x_ref, xp_ref, xn_ref, mod_ref, gpre_ref, gpost_ref, band_ref, invcc_ref, rowinv_ref, pw_ref, pb_ref,
         ps_ref, o_ref, hext_scr, rs_scr) = refs
    else:
        (x_ref, mod_ref, gpre_ref, gpost_ref, band_ref, invcc_ref, rowinv_ref, pw_ref, pb_ref, ps_ref,
         o_ref, hext_scr, rs_scr) = refs
    j = pl.program_id(1)
    rt = tm // width
    shift, scale, gate = mod_ref[0, 0:1, :], mod_ref[0, 1:2, :], mod_ref[0, 2:3, :]
    gm = gpre_ref[...] * (1.0 + scale)

    def prep(v):
        return _norm_mod(v, gm, shift)

    base = halo_rows * width
    hext_scr[base:base + tm] = prep(x_ref[0])
    def halo_row(src_ref, i, dst_row, reach, valid):
        groups = [g for g, w in enumerate(POOL_WINDOWS) if w // 2 >= reach]
        if groups:
            lanes = slice(groups[0] * POOL_GROUP, D_MODEL)
            v = src_ref[0, i * width:(i + 1) * width, :]
            r = lax.rsqrt(jnp.mean(v * v, axis=-1, keepdims=True) + EPS)
            y = (v[:, lanes] * r) * gm[:, lanes] + shift[:, lanes]
            hext_scr[dst_row * width:(dst_row + 1) * width, lanes] = jnp.where(valid, y, 0.0)

    for i in range(halo_rows):
        halo_row(xp_ref, i, i, halo_rows - i, j > 0)
        halo_row(xn_ref, i, halo_rows + rt + i, i + 2, j < nt - 1)

    def grid_row(r, ln):
        return hext_scr[(r + halo_rows) * width:(r + halo_rows + 1) * width, ln]

    for g, w in enumerate(POOL_WINDOWS):
        ln = slice(g * POOL_GROUP, (g + 1) * POOL_GROUP)
        if halo_rows:
            win = grid_row(-(w // 2), ln)
            for dr in range(-(w // 2) + 1, w // 2):
                win = win + grid_row(dr, ln)
        else:
            win = grid_row(0, ln)
        for r in range(rt):
            rs_scr[r * width:(r + 1) * width, ln] = win * rowinv_ref[pl.ds(j * rt + r, 1), ln]
            if r + 1 < rt:
                win = win + grid_row(r + w // 2, ln) - grid_row(r - w // 2, ln)

    blocks = [(ch, g) for ch in range(tm // POOL_CHUNK) for g in range(len(POOL_WINDOWS))]

    def window_sums(ch, g):
        ln = slice(g * POOL_GROUP, (g + 1) * POOL_GROUP)
        rs = rs_scr[ch * POOL_CHUNK:(ch + 1) * POOL_CHUNK, ln]
        hi = rs.astype(BF16)
        lo = (rs - hi.astype(F32)).astype(BF16)
        return (jnp.dot(band_ref[g], hi, preferred_element_type=F32)
                + jnp.dot(band_ref[g], lo, preferred_element_type=F32))

    def group_map(ch, g, sums):
        ln = slice(g * POOL_GROUP, (g + 1) * POOL_GROUP)
        hg = hext_scr[base + ch * POOL_CHUNK:base + (ch + 1) * POOL_CHUNK, ln]
        yg = jnp.dot((sums * invcc_ref[:, ln] - hg).astype(BF16), pw_ref[g], preferred_element_type=F32)
        rs_scr[ch * POOL_CHUNK:(ch + 1) * POOL_CHUNK, ln] = (yg + pb_ref[:, ln]) * ps_ref[:, ln]

    sums = window_sums(*blocks[0])
    for k, blk in enumerate(blocks):
        nxt = window_sums(*blocks[k + 1]) if k + 1 < len(blocks) else None
        group_map(*blk, sums)
        sums = nxt

    o_ref[0] = x_ref[0] + _norm_mod(rs_scr[...], gate * gpost_ref[...], None)


def _pool_consts(n_rows, width):
    idx = np.arange(POOL_CHUNK)
    row, col = idx // width, idx % width
    band = np.zeros((len(POOL_WINDOWS), POOL_CHUNK, POOL_CHUNK), np.float32)
    invcc = np.zeros((POOL_CHUNK, D_MODEL), np.float32)
    rowinv = np.zeros((n_rows, D_MODEL), np.float32)
    r = np.arange(n_rows)
    for g, w in enumerate(POOL_WINDOWS):
        same = row[:, None] == row[None, :]
        d = col[None, :] - col[:, None]
        band[g] = (same & (d >= -(w // 2)) & (d < w // 2)).astype(np.float32)
        cc = np.minimum(col + w // 2, width) - np.maximum(col - w // 2, 0)
        invcc[:, g * POOL_GROUP:(g + 1) * POOL_GROUP] = (1.0 / cc)[:, None]
        cr = np.minimum(r + w // 2, n_rows) - np.maximum(r - w // 2, 0)
        rowinv[:, g * POOL_GROUP:(g + 1) * POOL_GROUP] = (1.0 / cr)[:, None]
    return jnp.asarray(band, BF16), jnp.asarray(invcc), jnp.asarray(rowinv)


def _pool(x, mod, g_pre, g_post, pw, pb, ps, *, width, tm):
    b, t, d = x.shape
    n_rows = t // width
    nt = t // tm
    halo_rows = max(POOL_WINDOWS) // 2 if nt > 1 else 0
    assert nt > 1 or n_rows == 1
    band, invcc, rowinv = _pool_consts(n_rows, width)
    main = pl.BlockSpec((1, tm, d), lambda bb, j: (bb, j, 0))
    in_specs, args = [main], [x]
    if halo_rows:
        hb = halo_rows * width
        per = tm // hb
        last = t // hb - 1
        in_specs += [pl.BlockSpec((1, hb, d), lambda bb, j: (bb, jnp.maximum(j * per - 1, 0), 0)),
                     pl.BlockSpec((1, hb, d), lambda bb, j: (bb, jnp.minimum((j + 1) * per, last), 0))]
        args += [x, x]
    in_specs += [_mod_spec(mod.shape[0]), _const_spec((1, d)), _const_spec((1, d)), _const_spec(band.shape),
                 _const_spec(invcc.shape), _const_spec(rowinv.shape), _const_spec(pw.shape),
                 _const_spec((1, d)), _const_spec((1, d))]
    args += [mod, g_pre, g_post, band, invcc, rowinv, pw, pb, ps]
    kern = functools.partial(_pool_kernel, tm=tm, nt=nt, width=width, halo_rows=halo_rows)
    return pl.pallas_call(
        kern,
        grid=(b, nt),
        in_specs=in_specs,
        out_specs=main,
        out_shape=jax.ShapeDtypeStruct(x.shape, F32),
        scratch_shapes=[pltpu.VMEM((tm + 2 * halo_rows * width, d), F32), pltpu.VMEM((tm, d), F32)],
        compiler_params=pltpu.CompilerParams(dimension_semantics=("parallel", "parallel"),
                                             vmem_limit_bytes=VMEM_LIMIT),
        name="pool_mix",
    )(*args)


def kernel(x, c, ctx, c_ctx, ada_w, ada_b, norm_pre_mix, norm_post_mix, norm_pre_ffn, norm_post_ffn, rg_w_in,
           rg_conv_w, rg_conv_b, rg_gate_a_w, rg_gate_a_b, rg_gate_x_w, rg_gate_x_b, rg_lambda, rg_w_out, pool_w,
           pool_b, pool_scale, ffn_w_up, ffn_conv_w, ffn_conv_b, ffn_w_down):
    bsz, seq, d = x.shape
    ctx_len = ctx.shape[1]
    depth = ada_w.shape[0]
    assert d == D_MODEL and seq % GRID_W == 0

    ct = jnp.zeros((d, SUBLANES), F32).at[:, :bsz].set(c.T).at[:, bsz].set(c_ctx)
    mods = _ada(ct, ada_w, ada_b, bsz + 1)
    row = lambda a, i: a[i].reshape(1, -1)
    tm_lat = 512
    zeros_state = jnp.zeros((bsz, SUBLANES, d), F32)

    for i in range(depth):
        last = i == depth - 1
        mod_l = mods[i, :bsz].reshape(bsz, N_MOD, d)
        mod_c = mods[i, bsz].reshape(1, N_MOD, d)
        g_pre, g_post = row(norm_pre_mix, i), row(norm_post_mix, i)
        jm = i // 2
        if i % 2 == 0:
            w_gate = rg_w_in[jm, :, :d].astype(BF16)
            w_rec = rg_w_in[jm, :, d:].astype(BF16)
            w_out = rg_w_out[jm].astype(BF16)
            conv_w, conv_b = rg_conv_w[jm], row(rg_conv_b, jm)
            dirs = []
            for dd in range(2):
                wg = jnp.concatenate([rg_gate_a_w[jm, dd], rg_gate_x_w[jm, dd]], axis=-1)
                zg = jnp.zeros_like(wg[0::2])
                wg = jnp.concatenate([jnp.concatenate([wg[0::2], zg], axis=-1),
                                      jnp.concatenate([zg, wg[1::2]], axis=-1)], axis=1).astype(BF16)
                dirs.append((wg, rg_gate_a_b[jm, dd].reshape(1, d), rg_gate_x_b[jm, dd].reshape(1, d),
                             rg_lambda[jm, dd].reshape(1, d)))
            hb_c, uh_c, end_b = _rg_reverse(ctx, mod_c, g_pre, w_rec, conv_w, conv_b, *dirs[1], zeros_state,
                                            tm=ctx_len)
            ctx_new, end_f = _rg_forward(ctx, mod_c, g_pre, uh_c, *dirs[0], zeros_state, hb_c, w_gate, w_out,
                                         g_post, tm=ctx_len)
            hb_l, uh_l, _ = _rg_reverse(x, mod_l, g_pre, w_rec, conv_w, conv_b, *dirs[1], end_b, tm=tm_lat)
            x, _ = _rg_forward(x, mod_l, g_pre, uh_l, *dirs[0], end_f, hb_l, w_gate, w_out, g_post, tm=tm_lat)
            if not last:
                ctx = ctx_new
        else:
            pw = pool_w[jm].astype(BF16)
            pb, ps = row(pool_b, jm), row(pool_scale, jm)
            x_new = _pool(x, mod_l, g_pre, g_post, pw, pb, ps, width=GRID_W, tm=1024)
            if not last:
                ctx = _pool(ctx, mod_c, g_pre, g_post, pw, pb, ps, width=ctx_len, tm=ctx_len)
            x = x_new

        g_pre, g_post = row(norm_pre_ffn, i), row(norm_post_ffn, i)
        w_up, w_dn = ffn_w_up[i].astype(BF16), ffn_w_down[i].astype(BF16)
        cw, cb = ffn_conv_w[i], row(ffn_conv_b, i)
        x = _ffn(x, mod_l, g_pre, g_post, w_up, cw, cb, w_dn, tm=1024)
        if not last:
            ctx = _ffn(ctx, mod_c, g_pre, g_post, w_up, cw, cb, w_dn, tm=ctx_len)
    return x
```

```python
import functools
import math

import numpy as np
import jax
import jax.numpy as jnp
from jax import lax
from jax.experimental import pallas as pl
from jax.experimental.pallas import tpu as pltpu

D_MODEL = 1024
GRID_W = 64
N_MOD = 6
RG_HEADS = 8
RG_BLOCK = D_MODEL // RG_HEADS
RG_CONV = 4
RG_C = 8.0
POOL_WINDOWS = (2, 4, 8, 16)
POOL_GROUP = D_MODEL // len(POOL_WINDOWS)
D_FF = 3 * D_MODEL
EPS = 1e-6
TINY = 1e-30

SUBLANES = 8
LANES = 128
BF16_ROWS = 16
MXU_N = 512
HEADS_PER_TILE = MXU_N // RG_BLOCK
HALO = 8
POOL_CHUNK = 256
OUT_PARTS = 4
SCAN_SEGS = 16
VMEM_LIMIT = 60 * 1024 * 1024

F32 = jnp.float32
BF16 = jnp.bfloat16


def _norm_mod(v, gm, shift):
    r = lax.rsqrt(jnp.mean(v * v, axis=-1, keepdims=True) + EPS)
    y = (v * r) * gm
    return y if shift is None else y + shift


def _gelu(x):
    k1 = math.sqrt(2.0 / math.pi)
    hx = 0.5 * x
    return hx + hx * jnp.tanh(x * (k1 + (k1 * 0.044715) * (x * x)))


def _sigmoid(x):
    return 1.0 / (1.0 + jnp.exp(-x))


def _const_spec(shape):
    nd = len(shape)
    return pl.BlockSpec(shape, lambda *_: (0,) * nd, pipeline_mode=pl.Buffered(1))


def _tile_specs(tm, nt, reverse):
    per = tm // HALO
    last = nt * per - 1

    def tile(j):
        return (nt - 1 - j) if reverse else j

    main = pl.BlockSpec((1, tm, D_MODEL), lambda b, j: (b, tile(j), 0))
    prev = pl.BlockSpec((1, HALO, D_MODEL), lambda b, j: (b, jnp.maximum(tile(j) * per - 1, 0), 0))
    nxt = pl.BlockSpec((1, HALO, D_MODEL), lambda b, j: (b, jnp.minimum((tile(j) + 1) * per, last), 0))
    return main, prev, nxt


def _mod_spec(n_mod_rows):
    if n_mod_rows == 1:
        return pl.BlockSpec((1, N_MOD, D_MODEL), lambda b, j: (0, 0, 0))
    return pl.BlockSpec((1, N_MOD, D_MODEL), lambda b, j: (b, 0, 0))


def _ada_kernel(ct_ref, w_ref, b_ref, o_ref, *, n_rows):
    ct = ct_ref[...]
    st = ct * _sigmoid(ct)
    w = w_ref[0]
    rows = [jnp.sum(w * st[:, m:m + 1], axis=0, keepdims=True) for m in range(n_rows)]
    rows += [jnp.zeros_like(rows[0])] * (SUBLANES - n_rows)
    o_ref[0] = jnp.concatenate(rows, axis=0) + b_ref[0]


def _ada(ct, ada_w, ada_b, n_rows):
    depth, d, n = ada_w.shape
    tn = 1536
    return pl.pallas_call(
        functools.partial(_ada_kernel, n_rows=n_rows),
        grid=(depth, n // tn),
        in_specs=[pl.BlockSpec((d, SUBLANES), lambda i, k: (0, 0)),
                  pl.BlockSpec((1, d, tn), lambda i, k: (i, 0, k)),
                  pl.BlockSpec((1, 1, tn), lambda i, k: (i, 0, k))],
        out_specs=pl.BlockSpec((1, SUBLANES, tn), lambda i, k: (i, 0, k)),
        out_shape=jax.ShapeDtypeStruct((depth, SUBLANES, n), F32),
        compiler_params=pltpu.CompilerParams(dimension_semantics=("parallel", "parallel"),
                                             vmem_limit_bytes=VMEM_LIMIT),
        name="ada_mod",
    )(ct, ada_w, ada_b.reshape(depth, 1, n))


def _prep_ext(x_ref, xp_ref, xn_ref, g_pre, shift, scale, tile, nt):
    gm = g_pre * (1.0 + scale)

    def prep(v):
        return _norm_mod(v, gm, shift)

    hp = jnp.where(tile > 0, prep(xp_ref[0]), 0.0)
    hn = jnp.where(tile < nt - 1, prep(xn_ref[0]), 0.0)
    return jnp.concatenate([hp, prep(x_ref[0]), hn], axis=0)


def _shifted(slab_ref, slab, k, tm):
    if k == 0:
        return slab_ref[slab, pl.ds(HALO, tm), :]
    return slab_ref[slab, pl.ds(HALO + k, tm, stride=1), :]


def _ffn_kernel(x_ref, xp_ref, xn_ref, mod_ref, gpre_ref, gpost_ref, wup_ref, cw_ref, cb_ref, wdn_ref,
                o_ref, hext_ref, hm_ref, zs0_ref, zs1_ref, zv0_ref, zv1_ref, act_ref, *, tm, nt, fc):
    j = pl.program_id(1)
    shift, scale, gate = mod_ref[0, 3:4, :], mod_ref[0, 4:5, :], mod_ref[0, 5:6, :]
    hext = _prep_ext(x_ref, xp_ref, xn_ref, gpre_ref[...], shift, scale, j, nt)
    hext_ref[...] = hext.astype(BF16)
    hm_ref[...] = hext[HALO:HALO + tm].astype(BF16)
    ns = fc // LANES
    n_chunks = D_FF // fc
    zs_refs, zv_refs = (zs0_ref, zs1_ref), (zv0_ref, zv1_ref)

    def up(c):
        lo, zs_ref = c * fc, zs_refs[c % 2]
        zu = jnp.dot(hext_ref[...], wup_ref[:, lo:lo + fc], preferred_element_type=F32)
        for l in range(ns):
            zs_ref[l] = zu[:, l * LANES:(l + 1) * LANES]
        zv_refs[c % 2][...] = jnp.dot(hm_ref[...], wup_ref[:, D_FF + lo:D_FF + lo + fc],
                                      preferred_element_type=F32)

    def act(c):
        lo, zs_ref, zv_ref = c * fc, zs_refs[c % 2], zv_refs[c % 2]
        for l in range(ns):
            ln = slice(lo + l * LANES, lo + (l + 1) * LANES)
            cw = cw_ref[:, ln]
            u = (cw[0:1] * _shifted(zs_ref, l, -1, tm) + cw[1:2] * _shifted(zs_ref, l, 0, tm)
                 + cw[2:3] * _shifted(zs_ref, l, 1, tm) + cb_ref[:, ln])
            act_ref[:, ln] = (_gelu(u) * zv_ref[:, l * LANES:(l + 1) * LANES]).astype(BF16)

    up(0)
    for c in range(n_chunks):
        if c + 1 < n_chunks:
            up(c + 1)
        act(c)
    out = jnp.dot(act_ref[...], wdn_ref[...], preferred_element_type=F32)
    o_ref[0] = x_ref[0] + _norm_mod(out, gate * gpost_ref[...], None)


def _ffn(x, mod, g_pre, g_post, w_up, conv_w, conv_b, w_down, *, tm, fc=512):
    b, t, d = x.shape
    nt = t // tm
    main, prev, nxt = _tile_specs(tm, nt, False)
    kern = functools.partial(_ffn_kernel, tm=tm, nt=nt, fc=fc)
    return pl.pallas_call(
        kern,
        grid=(b, nt),
        in_specs=[main, prev, nxt, _mod_spec(mod.shape[0]),
                  _const_spec((1, d)), _const_spec((1, d)),
                  _const_spec(w_up.shape), _const_spec(conv_w.shape), _const_spec((1, D_FF)),
                  _const_spec(w_down.shape)],
        out_specs=pl.BlockSpec((1, tm, d), lambda bb, j: (bb, j, 0)),
        out_shape=jax.ShapeDtypeStruct(x.shape, F32),
        scratch_shapes=[pltpu.VMEM((tm + 2 * HALO, d), BF16), pltpu.VMEM((tm, d), BF16),
                        pltpu.VMEM((fc // LANES, tm + 2 * HALO, LANES), F32),
                        pltpu.VMEM((fc // LANES, tm + 2 * HALO, LANES), F32),
                        pltpu.VMEM((tm, fc), F32), pltpu.VMEM((tm, fc), F32), pltpu.VMEM((tm, D_FF), BF16)],
        compiler_params=pltpu.CompilerParams(dimension_semantics=("parallel", "parallel"),
                                             vmem_limit_bytes=VMEM_LIMIT),
        name="conv_ffn",
    )(x, x, x, mod, g_pre, g_post, w_up, conv_w, conv_b, w_down)


def _seg_pitch(seg):
    p = seg + 4
    assert p % 8 == 4, seg
    return p


def _scan_tile(a_scr, b_scr, h_scr, carry_ref, *, seg, reverse):
    pitch = _seg_pitch(seg)
    nh = a_scr.shape[0]
    nv = SCAN_SEGS // SUBLANES
    sub = lax.broadcasted_iota(jnp.int32, (SUBLANES, RG_BLOCK), 0)
    chains = [(hd, v) for hd in range(nh) for v in range(nv)]

    def rows(v, i):
        return pl.ds(v * SUBLANES * pitch + ((seg - 1 - i) if reverse else i), SUBLANES, stride=pitch)

    def pass1(i, st):
        out = []
        for k, (hd, v) in enumerate(chains):
            av = a_scr[hd, rows(v, i), :]
            out += [av * st[2 * k] + b_scr[hd, rows(v, i), :], st[2 * k + 1] * av]
        return tuple(out)

    init = tuple(jnp.zeros((SUBLANES, RG_BLOCK), F32) if k % 2 == 0 else jnp.ones((SUBLANES, RG_BLOCK), F32)
                 for k in range(2 * len(chains)))
    st = init
    for i in range(seg):
        st = pass1(i, st)

    first = SUBLANES - 1 if reverse else 0
    last = 0 if reverse else SUBLANES - 1
    cins = {}
    for hd in range(nh):
        carry = jnp.broadcast_to(carry_ref[hd], (SUBLANES, RG_BLOCK))
        for v in (reversed(range(nv)) if reverse else range(nv)):
            k = hd * nv + v
            e, pe = st[2 * k], st[2 * k + 1]
            cin = carry
            for _ in range(SUBLANES - 1):
                nxt = pltpu.roll(e + pe * cin, (SUBLANES - 1) if reverse else 1, axis=0)
                cin = jnp.where(sub == first, carry, nxt)
            cins[(hd, v)] = cin
            end = e + pe * cin
            carry = jnp.broadcast_to(end[last:last + 1, :], (SUBLANES, RG_BLOCK))
        carry_ref[hd] = carry[0:1, :]

    def pass2(i, hs):
        out = []
        for k, (hd, v) in enumerate(chains):
            h = a_scr[hd, rows(v, i), :] * hs[k] + b_scr[hd, rows(v, i), :]
            h_scr[hd, rows(v, i), :] = h
            out.append(h)
        return tuple(out)

    hs = tuple(cins[ch] for ch in chains)
    for i in range(seg):
        hs = pass2(i, hs)


def _rg_gates_scan(uh_of_head, after_gate_matmul, wg_ref, ba_ref, bx_ref, lam_ref, h0_ref, carry_out_ref,
                   a_scr, b_scr, h_scr, carry_ref, *, tm, nt, reverse):
    j = pl.program_id(1)
    seg = tm // SCAN_SEGS
    pitch = _seg_pitch(seg)

    @pl.when(j == 0)
    def _():
        for hd in range(RG_HEADS):
            carry_ref[hd] = h0_ref[0, 0:1, hd * RG_BLOCK:(hd + 1) * RG_BLOCK]

    lam = lam_ref[...]
    sp = jnp.maximum(-lam, 0.0) + jnp.log(1.0 + jnp.exp(-jnp.abs(lam)))
    nc = (-0.5 * RG_C * math.log2(math.e)) * sp
    half_ba, half_bx = 0.5 * ba_ref[...], 0.5 * bx_ref[...]
    def gate_matmul(pair):
        uhs = [uh_of_head(2 * pair + q) for q in range(2)]
        gz = jnp.dot(jnp.concatenate(uhs, axis=1).astype(BF16), wg_ref[pair], preferred_element_type=F32)
        after_gate_matmul(2 * pair + 1)
        return uhs, gz

    n_pairs = RG_HEADS // 2
    queued = gate_matmul(0)
    for pair in range(n_pairs):
        uhs, gz = queued
        if pair + 1 < n_pairs:
            queued = gate_matmul(pair + 1)
        for q in range(2):
            hd, uh = 2 * pair + q, uhs[q]
            ln = slice(hd * RG_BLOCK, (hd + 1) * RG_BLOCK)
            tr = jnp.tanh(gz[:, 2 * q * RG_BLOCK:(2 * q + 1) * RG_BLOCK] + half_ba[:, ln])
            ti = jnp.tanh(gz[:, (2 * q + 1) * RG_BLOCK:(2 * q + 2) * RG_BLOCK] + half_bx[:, ln])
            a = jnp.exp2(nc[:, ln] + nc[:, ln] * tr)
            y1 = 1.0 - a * a
            mult = y1 * lax.rsqrt(jnp.maximum(y1, TINY))
            bb = mult * (uh + uh * ti)
            for s in range(SCAN_SEGS):
                a_scr[hd, pl.ds(s * pitch, seg, stride=1), :] = a[s * seg:(s + 1) * seg]
                b_scr[hd, pl.ds(s * pitch, seg, stride=1), :] = bb[s * seg:(s + 1) * seg]

    _scan_tile(a_scr, b_scr, h_scr, carry_ref, seg=seg, reverse=reverse)

    @pl.when(j == nt - 1)
    def _():
        for hd in range(RG_HEADS):
            carry_out_ref[0, :, hd * RG_BLOCK:(hd + 1) * RG_BLOCK] = jnp.broadcast_to(
                carry_ref[hd], (SUBLANES, RG_BLOCK))


def _head_h(h_scr, hd, tm):
    seg = tm // SCAN_SEGS
    pitch = _seg_pitch(seg)
    return jnp.concatenate([h_scr[hd, pl.ds(s * pitch, seg, stride=1), :] for s in range(SCAN_SEGS)], axis=0)


def _rg_scan_kernel(x_ref, xp_ref, xn_ref, mod_ref, gpre_ref, wr_ref, cw_ref, cb_ref, wg_ref, ba_ref, bx_ref,
                    lam_ref, h0_ref, h_ref, uh_ref, carry_out_ref, hext_scr, rec0_scr, rec1_scr, a_scr, b_scr,
                    h_scr, carry_ref, *, tm, nt):
    tile = nt - 1 - pl.program_id(1)
    shift, scale = mod_ref[0, 0:1, :], mod_ref[0, 1:2, :]
    gm = gpre_ref[...] * (1.0 + scale)
    half_cw, half_cb = 0.5 * cw_ref[...], 0.5 * cb_ref[...]

    rec_scrs = (rec0_scr, rec1_scr)
    assert len(rec_scrs) * HEADS_PER_TILE == RG_HEADS
    n_ext = tm + 2 * HALO
    rows_a = -(-(HALO + tm // 2) // BF16_ROWS) * BF16_ROWS

    def rec_chunk(p, lo, hi):
        r = jnp.dot(hext_scr[lo:hi], wr_ref[:, p * MXU_N:(p + 1) * MXU_N], preferred_element_type=F32)
        for q in range(HEADS_PER_TILE):
            rec_scrs[p][q, lo:hi] = r[:, q * RG_BLOCK:(q + 1) * RG_BLOCK]

    hp = jnp.where(tile > 0, _norm_mod(xp_ref[0], gm, shift), 0.0)
    hext_scr[:rows_a] = jnp.concatenate([hp, _norm_mod(x_ref[0, :rows_a - HALO], gm, shift)],
                                        axis=0).astype(BF16)
    rec_chunk(0, 0, rows_a)
    hn = jnp.where(tile < nt - 1, _norm_mod(xn_ref[0], gm, shift), 0.0)
    hext_scr[rows_a:] = jnp.concatenate([_norm_mod(x_ref[0, rows_a - HALO:], gm, shift), hn],
                                        axis=0).astype(BF16)
    rec_chunk(0, rows_a, n_ext)

    def after_gate_matmul(hd):
        p, q = divmod(hd, HEADS_PER_TILE)
        if q == HEADS_PER_TILE - 1 and p + 1 < len(rec_scrs):
            rec_chunk(p + 1, 0, n_ext)

    def uh_of_head(hd):
        p, q = divmod(hd, HEADS_PER_TILE)
        ln = slice(hd * RG_BLOCK, (hd + 1) * RG_BLOCK)
        uh = half_cb[:, ln] + sum(half_cw[k:k + 1, ln] * _shifted(rec_scrs[p], q, k - RG_CONV // 2, tm)
                                  for k in range(RG_CONV))
        uh_ref[0, :, ln] = uh
        return uh

    _rg_gates_scan(uh_of_head, after_gate_matmul, wg_ref, ba_ref, bx_ref, lam_ref, h0_ref, carry_out_ref,
                   a_scr, b_scr, h_scr, carry_ref, tm=tm, nt=nt, reverse=True)
    for hd in range(RG_HEADS):
        h_ref[0, :, hd * RG_BLOCK:(hd + 1) * RG_BLOCK] = _head_h(h_scr, hd, tm)


def _rg_final_kernel(x_ref, mod_ref, gpre_ref, uh_ref, wg_ref, ba_ref, bx_ref, lam_ref, h0_ref, hb_ref,
                     wgate_ref, wout_ref, gpost_ref, o_ref, carry_out_ref, a_scr, b_scr, h_scr, carry_ref,
                     hm_scr, gz_scr, gy_scr, *, tm, nt):
    shift, scale, gate = mod_ref[0, 0:1, :], mod_ref[0, 1:2, :], mod_ref[0, 2:3, :]
    hm_scr[...] = _norm_mod(x_ref[0], gpre_ref[...] * (1.0 + scale), shift).astype(BF16)

    def after_gate_matmul(hd):
        pass

    def uh_of_head(hd):
        if hd % 2 == 0:
            ln2 = slice(hd * RG_BLOCK, (hd + 2) * RG_BLOCK)
            gz_scr[:, ln2] = jnp.dot(hm_scr[...], wgate_ref[:, ln2], preferred_element_type=F32)
        return uh_ref[0, :, hd * RG_BLOCK:(hd + 1) * RG_BLOCK]

    _rg_gates_scan(uh_of_head, after_gate_matmul, wg_ref, ba_ref, bx_ref, lam_ref, h0_ref, carry_out_ref,
                   a_scr, b_scr, h_scr, carry_ref, tm=tm, nt=nt, reverse=False)
    out = None
    for part in range(OUT_PARTS):
        kk = slice(part * (D_MODEL // OUT_PARTS), (part + 1) * (D_MODEL // OUT_PARTS))
        for hd in range(part * RG_HEADS // OUT_PARTS, (part + 1) * RG_HEADS // OUT_PARTS):
            ln = slice(hd * RG_BLOCK, (hd + 1) * RG_BLOCK)
            gy_scr[:, ln] = (_gelu(gz_scr[:, ln]) * (_head_h(h_scr, hd, tm) + hb_ref[0, :, ln])).astype(BF16)
        partial = jnp.dot(gy_scr[:, kk], wout_ref[kk, :], preferred_element_type=F32)
        out = partial if out is None else out + partial
    o_ref[0] = x_ref[0] + _norm_mod(out, gate * gpost_ref[...], None)


def _scan_scratch(tm):
    rows = SCAN_SEGS * _seg_pitch(tm // SCAN_SEGS)
    return [pltpu.VMEM((RG_HEADS, rows, RG_BLOCK), F32)] * 3 + [pltpu.VMEM((RG_HEADS, 1, RG_BLOCK), F32)]


def _rg_reverse(x, mod, g_pre, w_rec, conv_w, conv_b, w_gates, b_a, b_x, lam, h0, *, tm):
    b, t, d = x.shape
    nt = t // tm
    main, prev, nxt = _tile_specs(tm, nt, True)
    carry_spec = pl.BlockSpec((1, SUBLANES, d), lambda bb, j: (bb, 0, 0))
    return pl.pallas_call(
        functools.partial(_rg_scan_kernel, tm=tm, nt=nt),
        grid=(b, nt),
        in_specs=[main, prev, nxt, _mod_spec(mod.shape[0]), _const_spec((1, d)), _const_spec(w_rec.shape),
                  _const_spec(conv_w.shape), _const_spec((1, d)), _const_spec(w_gates.shape),
                  _const_spec((1, d)), _const_spec((1, d)), _const_spec((1, d)), carry_spec],
        out_specs=[main, main, carry_spec],
        out_shape=[jax.ShapeDtypeStruct(x.shape, F32), jax.ShapeDtypeStruct(x.shape, F32),
                   jax.ShapeDtypeStruct((b, SUBLANES, d), F32)],
        scratch_shapes=[pltpu.VMEM((tm + 2 * HALO, d), BF16),
                        pltpu.VMEM((HEADS_PER_TILE, tm + 2 * HALO, RG_BLOCK), F32),
                        pltpu.VMEM((HEADS_PER_TILE, tm + 2 * HALO, RG_BLOCK), F32)] + _scan_scratch(tm),
        compiler_params=pltpu.CompilerParams(dimension_semantics=("arbitrary", "arbitrary"),
                                             vmem_limit_bytes=VMEM_LIMIT),
        name="rg_scan",
    )(x, x, x, mod, g_pre, w_rec, conv_w, conv_b, w_gates, b_a, b_x, lam, h0)


def _rg_forward(x, mod, g_pre, uh, w_gates, b_a, b_x, lam, h0, h_rev, w_gate, w_out, g_post, *, tm):
    b, t, d = x.shape
    nt = t // tm
    main = pl.BlockSpec((1, tm, d), lambda bb, j: (bb, j, 0))
    carry_spec = pl.BlockSpec((1, SUBLANES, d), lambda bb, j: (bb, 0, 0))
    return pl.pallas_call(
        functools.partial(_rg_final_kernel, tm=tm, nt=nt),
        grid=(b, nt),
        in_specs=[main, _mod_spec(mod.shape[0]), _const_spec((1, d)), main, _const_spec(w_gates.shape),
                  _const_spec((1, d)), _const_spec((1, d)), _const_spec((1, d)), carry_spec, main,
                  _const_spec(w_gate.shape), _const_spec(w_out.shape), _const_spec((1, d))],
        out_specs=[main, carry_spec],
        out_shape=[jax.ShapeDtypeStruct(x.shape, F32), jax.ShapeDtypeStruct((b, SUBLANES, d), F32)],
        scratch_shapes=_scan_scratch(tm) + [pltpu.VMEM((tm, d), BF16), pltpu.VMEM((tm, d), F32),
                                            pltpu.VMEM((tm, d), BF16)],
        compiler_params=pltpu.CompilerParams(dimension_semantics=("arbitrary", "arbitrary"),
                                             vmem_limit_bytes=VMEM_LIMIT),
        name="rg_final",
    )(x, mod, g_pre, uh, w_gates, b_a, b_x, lam, h0, h_rev, w_gate, w_out, g_post)


def _pool_kernel(*refs, tm, nt, width, halo_rows):
    if halo_rows:
        (x_ref, xp_ref, xn_ref, mod_ref, gpre_ref, gpost_ref, band_ref, invcc_ref, rowinv_ref, pw_ref, pb_ref,
         ps_ref, o_ref, hext_scr, rs_scr) = refs
    else:
        (x_ref, mod_ref, gpre_ref, gpost_ref, band_ref, invcc_ref, rowinv_ref, pw_ref, pb_ref, ps_ref,
         o_ref, hext_scr, rs_scr) = refs
    j = pl.program_id(1)
    rt = tm // width
    shift, scale, gate = mod_ref[0, 0:1, :], mod_ref[0, 1:2, :], mod_ref[0, 2:3, :]
    gm = gpre_ref[...] * (1.0 + scale)

    def prep(v):
        return _norm_mod(v, gm, shift)

    base = halo_rows * width
    hext_scr[base:base + tm] = prep(x_ref[0])
    def halo_row(src_ref, i, dst_row, reach, valid):
        groups = [g for g, w in enumerate(POOL_WINDOWS) if w // 2 >= reach]
        if groups:
            lanes = slice(groups[0] * POOL_GROUP, D_MODEL)
            v = src_ref[0, i * width:(i + 1) * width, :]
            r = lax.rsqrt(jnp.mean(v * v, axis=-1, keepdims=True) + EPS)
            y = (v[:, lanes] * r) * gm[:, lanes] + shift[:, lanes]
            hext_scr[dst_row * width:(dst_row + 1) * width, lanes] = jnp.where(valid, y, 0.0)

    for i in range(halo_rows):
        halo_row(xp_ref, i, i, halo_rows - i, j > 0)
        halo_row(xn_ref, i, halo_rows + rt + i, i + 2, j < nt - 1)

    def grid_row(r, ln):
        return hext_scr[(r + halo_rows) * width:(r + halo_rows + 1) * width, ln]

    for g, w in enumerate(POOL_WINDOWS):
        ln = slice(g * POOL_GROUP, (g + 1) * POOL_GROUP)
        if halo_rows:
            win = grid_row(-(w // 2), ln)
            for dr in range(-(w // 2) + 1, w // 2):
                win = win + grid_row(dr, ln)
        else:
            win = grid_row(0, ln)
        for r in range(rt):
            rs_scr[r * width:(r + 1) * width, ln] = win * rowinv_ref[pl.ds(j * rt + r, 1), ln]
            if r + 1 < rt:
                win = win + grid_row(r + w // 2, ln) - grid_row(r - w // 2, ln)

    blocks = [(ch, g) for ch in range(tm // POOL_CHUNK) for g in range(len(POOL_WINDOWS))]

    def window_sums(ch, g):
        ln = slice(g * POOL_GROUP, (g + 1) * POOL_GROUP)
        rs = rs_scr[ch * POOL_CHUNK:(ch + 1) * POOL_CHUNK, ln]
        hi = rs.astype(BF16)
        lo = (rs - hi.astype(F32)).astype(BF16)
        return (jnp.dot(band_ref[g], hi, preferred_element_type=F32)
                + jnp.dot(band_ref[g], lo, preferred_element_type=F32))

    def group_map(ch, g, sums):
        ln = slice(g * POOL_GROUP, (g + 1) * POOL_GROUP)
        hg = hext_scr[base + ch * POOL_CHUNK:base + (ch + 1) * POOL_CHUNK, ln]
        yg = jnp.dot((sums * invcc_ref[:, ln] - hg).astype(BF16), pw_ref[g], preferred_element_type=F32)
        rs_scr[ch * POOL_CHUNK:(ch + 1) * POOL_CHUNK, ln] = (yg + pb_ref[:, ln]) * ps_ref[:, ln]

    sums = window_sums(*blocks[0])
    for k, blk in enumerate(blocks):
        nxt = window_sums(*blocks[k + 1]) if k + 1 < len(blocks) else None
        group_map(*blk, sums)
        sums = nxt

    o_ref[0] = x_ref[0] + _norm_mod(rs_scr[...], gate * gpost_ref[...], None)


def _pool_consts(n_rows, width):
    idx = np.arange(POOL_CHUNK)
    row, col = idx // width, idx % width
    band = np.zeros((len(POOL_WINDOWS), POOL_CHUNK, POOL_CHUNK), np.float32)
    invcc = np.zeros((POOL_CHUNK, D_MODEL), np.float32)
    rowinv = np.zeros((n_rows, D_MODEL), np.float32)
    r = np.arange(n_rows)
    for g, w in enumerate(POOL_WINDOWS):
        same = row[:, None] == row[None, :]
        d = col[None, :] - col[:, None]
        band[g] = (same & (d >= -(w // 2)) & (d < w // 2)).astype(np.float32)
        cc = np.minimum(col + w // 2, width) - np.maximum(col - w // 2, 0)
        invcc[:, g * POOL_GROUP:(g + 1) * POOL_GROUP] = (1.0 / cc)[:, None]
        cr = np.minimum(r + w // 2, n_rows) - np.maximum(r - w // 2, 0)
        rowinv[:, g * POOL_GROUP:(g + 1) * POOL_GROUP] = (1.0 / cr)[:, None]
    return jnp.asarray(band, BF16), jnp.asarray(invcc), jnp.asarray(rowinv)


def _pool(x, mod, g_pre, g_post, pw, pb, ps, *, width, tm):
    b, t, d = x.shape
    n_rows = t // width
    nt = t // tm
    halo_rows = max(POOL_WINDOWS) // 2 if nt > 1 else 0
    assert nt > 1 or n_rows == 1
    band, invcc, rowinv = _pool_consts(n_rows, width)
    main = pl.BlockSpec((1, tm, d), lambda bb, j: (bb, j, 0))
    in_specs, args = [main], [x]
    if halo_rows:
        hb = halo_rows * width
        per = tm // hb
        last = t // hb - 1
        in_specs += [pl.BlockSpec((1, hb, d), lambda bb, j: (bb, jnp.maximum(j * per - 1, 0), 0)),
                     pl.BlockSpec((1, hb, d), lambda bb, j: (bb, jnp.minimum((j + 1) * per, last), 0))]
        args += [x, x]
    in_specs += [_mod_spec(mod.shape[0]), _const_spec((1, d)), _const_spec((1, d)), _const_spec(band.shape),
                 _const_spec(invcc.shape), _const_spec(rowinv.shape), _const_spec(pw.shape),
                 _const_spec((1, d)), _const_spec((1, d))]
    args += [mod, g_pre, g_post, band, invcc, rowinv, pw, pb, ps]
    kern = functools.partial(_pool_kernel, tm=tm, nt=nt, width=width, halo_rows=halo_rows)
    return pl.pallas_call(
        kern,
        grid=(b, nt),
        in_specs=in_specs,
        out_specs=main,
        out_shape=jax.ShapeDtypeStruct(x.shape, F32),
        scratch_shapes=[pltpu.VMEM((tm + 2 * halo_rows * width, d), F32), pltpu.VMEM((tm, d), F32)],
        compiler_params=pltpu.CompilerParams(dimension_semantics=("parallel", "parallel"),
                                             vmem_limit_bytes=VMEM_LIMIT),
        name="pool_mix",
    )(*args)


def kernel(x, c, ctx, c_ctx, ada_w, ada_b, norm_pre_mix, norm_post_mix, norm_pre_ffn, norm_post_ffn, rg_w_in,
           rg_conv_w, rg_conv_b, rg_gate_a_w, rg_gate_a_b, rg_gate_x_w, rg_gate_x_b, rg_lambda, rg_w_out, pool_w,
           pool_b, pool_scale, ffn_w_up, ffn_conv_w, ffn_conv_b, ffn_w_down):
    bsz, seq, d = x.shape
    ctx_len = ctx.shape[1]
    depth = ada_w.shape[0]
    assert d == D_MODEL and seq % GRID_W == 0

    ct = jnp.zeros((d, SUBLANES), F32).at[:, :bsz].set(c.T).at[:, bsz].set(c_ctx)
    mods = _ada(ct, ada_w, ada_b, bsz + 1)
    row = lambda a, i: a[i].reshape(1, -1)
    tm_lat = 512
    zeros_state = jnp.zeros((bsz, SUBLANES, d), F32)

    for i in range(depth):
        last = i == depth - 1
        mod_l = mods[i, :bsz].reshape(bsz, N_MOD, d)
        mod_c = mods[i, bsz].reshape(1, N_MOD, d)
        g_pre, g_post = row(norm_pre_mix, i), row(norm_post_mix, i)
        jm = i // 2
        if i % 2 == 0:
            w_gate = rg_w_in[jm, :, :d].astype(BF16)
            w_rec = rg_w_in[jm, :, d:].astype(BF16)
            w_out = rg_w_out[jm].astype(BF16)
            conv_w, conv_b = rg_conv_w[jm], row(rg_conv_b, jm)
            dirs = []
            for dd in range(2):
                wg = jnp.concatenate([rg_gate_a_w[jm, dd], rg_gate_x_w[jm, dd]], axis=-1)
                zg = jnp.zeros_like(wg[0::2])
                wg = jnp.concatenate([jnp.concatenate([wg[0::2], zg], axis=-1),
                                      jnp.concatenate([zg, wg[1::2]], axis=-1)], axis=1).astype(BF16)
                dirs.append((wg, rg_gate_a_b[jm, dd].reshape(1, d), rg_gate_x_b[jm, dd].reshape(1, d),
                             rg_lambda[jm, dd].reshape(1, d)))
            hb_c, uh_c, end_b = _rg_reverse(ctx, mod_c, g_pre, w_rec, conv_w, conv_b, *dirs[1], zeros_state,
                                            tm=ctx_len)
            ctx_new, end_f = _rg_forward(ctx, mod_c, g_pre, uh_c, *dirs[0], zeros_state, hb_c, w_gate, w_out,
                                         g_post, tm=ctx_len)
            hb_l, uh_l, _ = _rg_reverse(x, mod_l, g_pre, w_rec, conv_w, conv_b, *dirs[1], end_b, tm=tm_lat)
            x, _ = _rg_forward(x, mod_l, g_pre, uh_l, *dirs[0], end_f, hb_l, w_gate, w_out, g_post, tm=tm_lat)
            if not last:
                ctx = ctx_new
        else:
            pw = pool_w[jm].astype(BF16)
            pb, ps = row(pool_b, jm), row(pool_scale, jm)
            x_new = _pool(x, mod_l, g_pre, g_post, pw, pb, ps, width=GRID_W, tm=1024)
            if not last:
                ctx = _pool(ctx, mod_c, g_pre, g_post, pw, pb, ps, width=ctx_len, tm=ctx_len)
            x = x_new

        g_pre, g_post = row(norm_pre_ffn, i), row(norm_post_ffn, i)
        w_up, w_dn = ffn_w_up[i].astype(BF16), ffn_w_down[i].astype(BF16)
        cw, cb = ffn_conv_w[i], row(ffn_conv_b, i)
        x = _ffn(x, mod_l, g_pre, g_post, w_up, cw, cb, w_dn, tm=1024)
        if not last:
            ctx = _ffn(ctx, mod_c, g_pre, g_post, w_up, cw, cb, w_dn, tm=ctx_len)
    return x
```

```python
import functools
import math

import numpy as np
import jax
import jax.numpy as jnp
from jax import lax
from jax.experimental import pallas as pl
from jax.experimental.pallas import tpu as pltpu

D_MODEL = 1024
GRID_W = 64
N_MOD = 6
RG_HEADS = 8
RG_BLOCK = D_MODEL // RG_HEADS
RG_CONV = 4
RG_C = 8.0
POOL_WINDOWS = (2, 4, 8, 16)
POOL_GROUP = D_MODEL // len(POOL_WINDOWS)
D_FF = 3 * D_MODEL
EPS = 1e-6
TINY = 1e-30

SUBLANES = 8
LANES = 128
BF16_ROWS = 16
MXU_N = 512
HEADS_PER_TILE = MXU_N // RG_BLOCK
HALO = 8
POOL_CHUNK = 256
OUT_PARTS = 4
SCAN_SEGS = 16
VMEM_LIMIT = 60 * 1024 * 1024

F32 = jnp.float32
BF16 = jnp.bfloat16


def _norm_mod(v, gm, shift):
    r = lax.rsqrt(jnp.mean(v * v, axis=-1, keepdims=True) + EPS)
    y = (v * r) * gm
    return y if shift is None else y + shift


def _gelu(x):
    k1 = math.sqrt(2.0 / math.pi)
    hx = 0.5 * x
    return hx + hx * jnp.tanh(x * (k1 + (k1 * 0.044715) * (x * x)))


def _sigmoid(x):
    return 1.0 / (1.0 + jnp.exp(-x))


def _const_spec(shape):
    nd = len(shape)
    return pl.BlockSpec(shape, lambda *_: (0,) * nd, pipeline_mode=pl.Buffered(1))


def _tile_specs(tm, nt, reverse):
    per = tm // HALO
    last = nt * per - 1

    def tile(j):
        return (nt - 1 - j) if reverse else j

    main = pl.BlockSpec((1, tm, D_MODEL), lambda b, j: (b, tile(j), 0))
    prev = pl.BlockSpec((1, HALO, D_MODEL), lambda b, j: (b, jnp.maximum(tile(j) * per - 1, 0), 0))
    nxt = pl.BlockSpec((1, HALO, D_MODEL), lambda b, j: (b, jnp.minimum((tile(j) + 1) * per, last), 0))
    return main, prev, nxt


def _mod_spec(n_mod_rows):
    if n_mod_rows == 1:
        return pl.BlockSpec((1, N_MOD, D_MODEL), lambda b, j: (0, 0, 0))
    return pl.BlockSpec((1, N_MOD, D_MODEL), lambda b, j: (b, 0, 0))


def _ada_kernel(ct_ref, w_ref, b_ref, o_ref, *, n_rows):
    ct = ct_ref[...]
    st = ct * _sigmoid(ct)
    w = w_ref[0]
    rows = [jnp.sum(w * st[:, m:m + 1], axis=0, keepdims=True) for m in range(n_rows)]
    rows += [jnp.zeros_like(rows[0])] * (SUBLANES - n_rows)
    o_ref[0] = jnp.concatenate(rows, axis=0) + b_ref[0]


def _ada(ct, ada_w, ada_b, n_rows):
    depth, d, n = ada_w.shape
    tn = 1536
    return pl.pallas_call(
        functools.partial(_ada_kernel, n_rows=n_rows),
        grid=(depth, n // tn),
        in_specs=[pl.BlockSpec((d, SUBLANES), lambda i, k: (0, 0)),
                  pl.BlockSpec((1, d, tn), lambda i, k: (i, 0, k)),
                  pl.BlockSpec((1, 1, tn), lambda i, k: (i, 0, k))],
        out_specs=pl.BlockSpec((1, SUBLANES, tn), lambda i, k: (i, 0, k)),
        out_shape=jax.ShapeDtypeStruct((depth, SUBLANES, n), F32),
        compiler_params=pltpu.CompilerParams(dimension_semantics=("parallel", "parallel"),
                                             vmem_limit_bytes=VMEM_LIMIT),
        name="ada_mod",
    )(ct, ada_w, ada_b.reshape(depth, 1, n))


def _prep_ext(x_ref, xp_ref, xn_ref, g_pre, shift, scale, tile, nt):
    gm = g_pre * (1.0 + scale)

    def prep(v):
        return _norm_mod(v, gm, shift)

    hp = jnp.where(tile > 0, prep(xp_ref[0]), 0.0)
    hn = jnp.where(tile < nt - 1, prep(xn_ref[0]), 0.0)
    return jnp.concatenate([hp, prep(x_ref[0]), hn], axis=0)


def _shifted(slab_ref, slab, k, tm):
    if k == 0:
        return slab_ref[slab, pl.ds(HALO, tm), :]
    return slab_ref[slab, pl.ds(HALO + k, tm, stride=1), :]


def _ffn_kernel(x_ref, xp_ref, xn_ref, mod_ref, gpre_ref, gpost_ref, wup_ref, cw_ref, cb_ref, wdn_ref,
                o_ref, hext_ref, hm_ref, zs0_ref, zs1_ref, zv0_ref, zv1_ref, act_ref, *, tm, nt, fc):
    j = pl.program_id(1)
    shift, scale, gate = mod_ref[0, 3:4, :], mod_ref[0, 4:5, :], mod_ref[0, 5:6, :]
    hext = _prep_ext(x_ref, xp_ref, xn_ref, gpre_ref[...], shift, scale, j, nt)
    hext_ref[...] = hext.astype(BF16)
    hm_ref[...] = hext[HALO:HALO + tm].astype(BF16)
    ns = fc // LANES
    n_chunks = D_FF // fc
    zs_refs, zv_refs = (zs0_ref, zs1_ref), (zv0_ref, zv1_ref)

    def up(c):
        lo, zs_ref = c * fc, zs_refs[c % 2]
        zu = jnp.dot(hext_ref[...], wup_ref[:, lo:lo + fc], preferred_element_type=F32)
        for l in range(ns):
            zs_ref[l] = zu[:, l * LANES:(l + 1) * LANES]
        zv_refs[c % 2][...] = jnp.dot(hm_ref[...], wup_ref[:, D_FF + lo:D_FF + lo + fc],
                                      preferred_element_type=F32)

    def act(c):
        lo, zs_ref, zv_ref = c * fc, zs_refs[c % 2], zv_refs[c % 2]
        for l in range(ns):
            ln = slice(lo + l * LANES, lo + (l + 1) * LANES)
            cw = cw_ref[:, ln]
            u = (cw[0:1] * _shifted(zs_ref, l, -1, tm) + cw[1:2] * _shifted(zs_ref, l, 0, tm)
                 + cw[2:3] * _shifted(zs_ref, l, 1, tm) + cb_ref[:, ln])
            act_ref[:, ln] = (_gelu(u) * zv_ref[:, l * LANES:(l + 1) * LANES]).astype(BF16)

    up(0)
    for c in range(n_chunks):
        if c + 1 < n_chunks:
            up(c + 1)
        act(c)
    out = jnp.dot(act_ref[...], wdn_ref[...], preferred_element_type=F32)
    o_ref[0] = x_ref[0] + _norm_mod(out, gate * gpost_ref[...], None)


def _ffn(x, mod, g_pre, g_post, w_up, conv_w, conv_b, w_down, *, tm, fc=1024):
    b, t, d = x.shape
    nt = t // tm
    main, prev, nxt = _tile_specs(tm, nt, False)
    kern = functools.partial(_ffn_kernel, tm=tm, nt=nt, fc=fc)
    return pl.pallas_call(
        kern,
        grid=(b, nt),
        in_specs=[main, prev, nxt, _mod_spec(mod.shape[0]),
                  _const_spec((1, d)), _const_spec((1, d)),
                  _const_spec(w_up.shape), _const_spec(conv_w.shape), _const_spec((1, D_FF)),
                  _const_spec(w_down.shape)],
        out_specs=pl.BlockSpec((1, tm, d), lambda bb, j: (bb, j, 0)),
        out_shape=jax.ShapeDtypeStruct(x.shape, F32),
        scratch_shapes=[pltpu.VMEM((tm + 2 * HALO, d), BF16), pltpu.VMEM((tm, d), BF16),
                        pltpu.VMEM((fc // LANES, tm + 2 * HALO, LANES), F32),
                        pltpu.VMEM((fc // LANES, tm + 2 * HALO, LANES), F32),
                        pltpu.VMEM((tm, fc), F32), pltpu.VMEM((tm, fc), F32), pltpu.VMEM((tm, D_FF), BF16)],
        compiler_params=pltpu.CompilerParams(dimension_semantics=("parallel", "parallel"),
                                             vmem_limit_bytes=VMEM_LIMIT),
        name="conv_ffn",
    )(x, x, x, mod, g_pre, g_post, w_up, conv_w, conv_b, w_down)


def _seg_pitch(seg):
    p = seg + 4
    assert p % 8 == 4, seg
    return p


def _scan_tile(a_scr, b_scr, h_scr, carry_ref, *, seg, reverse):
    pitch = _seg_pitch(seg)
    nh = a_scr.shape[0]
    nv = SCAN_SEGS // SUBLANES
    sub = lax.broadcasted_iota(jnp.int32, (SUBLANES, RG_BLOCK), 0)
    chains = [(hd, v) for hd in range(nh) for v in range(nv)]

    def rows(v, i):
        return pl.ds(v * SUBLANES * pitch + ((seg - 1 - i) if reverse else i), SUBLANES, stride=pitch)

    def pass1(i, st):
        out = []
        for k, (hd, v) in enumerate(chains):
            av = a_scr[hd, rows(v, i), :]
            out += [av * st[2 * k] + b_scr[hd, rows(v, i), :], st[2 * k + 1] * av]
        return tuple(out)

    init = tuple(jnp.zeros((SUBLANES, RG_BLOCK), F32) if k % 2 == 0 else jnp.ones((SUBLANES, RG_BLOCK), F32)
                 for k in range(2 * len(chains)))
    st = init
    for i in range(seg):
        st = pass1(i, st)

    first = SUBLANES - 1 if reverse else 0
    last = 0 if reverse else SUBLANES - 1
    cins = {}
    for hd in range(nh):
        carry = jnp.broadcast_to(carry_ref[hd], (SUBLANES, RG_BLOCK))
        for v in (reversed(range(nv)) if reverse else range(nv)):
            k = hd * nv + v
            e, pe = st[2 * k], st[2 * k + 1]
            cin = carry
            for _ in range(SUBLANES - 1):
                nxt = pltpu.roll(e + pe * cin, (SUBLANES - 1) if reverse else 1, axis=0)
                cin = jnp.where(sub == first, carry, nxt)
            cins[(hd, v)] = cin
            end = e + pe * cin
            carry = jnp.broadcast_to(end[last:last + 1, :], (SUBLANES, RG_BLOCK))
        carry_ref[hd] = carry[0:1, :]

    def pass2(i, hs):
        out = []
        for k, (hd, v) in enumerate(chains):
            h = a_scr[hd, rows(v, i), :] * hs[k] + b_scr[hd, rows(v, i), :]
            h_scr[hd, rows(v, i), :] = h
            out.append(h)
        return tuple(out)

    hs = tuple(cins[ch] for ch in chains)
    for i in range(seg):
        hs = pass2(i, hs)


def _rg_gates_scan(uh_of_head, after_gate_matmul, wg_ref, ba_ref, bx_ref, lam_ref, h0_ref, carry_out_ref,
                   a_scr, b_scr, h_scr, carry_ref, *, tm, nt, reverse):
    j = pl.program_id(1)
    seg = tm // SCAN_SEGS
    pitch = _seg_pitch(seg)

    @pl.when(j == 0)
    def _():
        for hd in range(RG_HEADS):
            carry_ref[hd] = h0_ref[0, 0:1, hd * RG_BLOCK:(hd + 1) * RG_BLOCK]

    lam = lam_ref[...]
    sp = jnp.maximum(-lam, 0.0) + jnp.log(1.0 + jnp.exp(-jnp.abs(lam)))
    nc = (-0.5 * RG_C * math.log2(math.e)) * sp
    half_ba, half_bx = 0.5 * ba_ref[...], 0.5 * bx_ref[...]
    def gate_matmul(pair):
        uhs = [uh_of_head(2 * pair + q) for q in range(2)]
        gz = jnp.dot(jnp.concatenate(uhs, axis=1).astype(BF16), wg_ref[pair], preferred_element_type=F32)
        after_gate_matmul(2 * pair + 1)
        return uhs, gz

    n_pairs = RG_HEADS // 2
    queued = gate_matmul(0)
    for pair in range(n_pairs):
        uhs, gz = queued
        if pair + 1 < n_pairs:
            queued = gate_matmul(pair + 1)
        for q in range(2):
            hd, uh = 2 * pair + q, uhs[q]
            ln = slice(hd * RG_BLOCK, (hd + 1) * RG_BLOCK)
            tr = jnp.tanh(gz[:, 2 * q * RG_BLOCK:(2 * q + 1) * RG_BLOCK] + half_ba[:, ln])
            ti = jnp.tanh(gz[:, (2 * q + 1) * RG_BLOCK:(2 * q + 2) * RG_BLOCK] + half_bx[:, ln])
            a = jnp.exp2(nc[:, ln] + nc[:, ln] * tr)
            y1 = 1.0 - a * a
            mult = y1 * lax.rsqrt(jnp.maximum(y1, TINY))
            bb = mult * (uh + uh * ti)
            for s in range(SCAN_SEGS):
                a_scr[hd, pl.ds(s * pitch, seg, stride=1), :] = a[s * seg:(s + 1) * seg]
                b_scr[hd, pl.ds(s * pitch, seg, stride=1), :] = bb[s * seg:(s + 1) * seg]

    _scan_tile(a_scr, b_scr, h_scr, carry_ref, seg=seg, reverse=reverse)

    @pl.when(j == nt - 1)
    def _():
        for hd in range(RG_HEADS):
            carry_out_ref[0, :, hd * RG_BLOCK:(hd + 1) * RG_BLOCK] = jnp.broadcast_to(
                carry_ref[hd], (SUBLANES, RG_BLOCK))


def _head_h(h_scr, hd, tm):
    seg = tm // SCAN_SEGS
    pitch = _seg_pitch(seg)
    return jnp.concatenate([h_scr[hd, pl.ds(s * pitch, seg, stride=1), :] for s in range(SCAN_SEGS)], axis=0)


def _rg_scan_kernel(x_ref, xp_ref, xn_ref, mod_ref, gpre_ref, wr_ref, cw_ref, cb_ref, wg_ref, ba_ref, bx_ref,
                    lam_ref, h0_ref, h_ref, uh_ref, carry_out_ref, hext_scr, rec0_scr, rec1_scr, a_scr, b_scr,
                    h_scr, carry_ref, *, tm, nt):
    tile = nt - 1 - pl.program_id(1)
    shift, scale = mod_ref[0, 0:1, :], mod_ref[0, 1:2, :]
    gm = gpre_ref[...] * (1.0 + scale)
    half_cw, half_cb = 0.5 * cw_ref[...], 0.5 * cb_ref[...]

    rec_scrs = (rec0_scr, rec1_scr)
    assert len(rec_scrs) * HEADS_PER_TILE == RG_HEADS
    n_ext = tm + 2 * HALO
    rows_a = -(-(HALO + tm // 2) // BF16_ROWS) * BF16_ROWS

    def rec_chunk(p, lo, hi):
        r = jnp.dot(hext_scr[lo:hi], wr_ref[:, p * MXU_N:(p + 1) * MXU_N], preferred_element_type=F32)
        for q in range(HEADS_PER_TILE):
            rec_scrs[p][q, lo:hi] = r[:, q * RG_BLOCK:(q + 1) * RG_BLOCK]

    hp = jnp.where(tile > 0, _norm_mod(xp_ref[0], gm, shift), 0.0)
    hext_scr[:rows_a] = jnp.concatenate([hp, _norm_mod(x_ref[0, :rows_a - HALO], gm, shift)],
                                        axis=0).astype(BF16)
    rec_chunk(0, 0, rows_a)
    hn = jnp.where(tile < nt - 1, _norm_mod(xn_ref[0], gm, shift), 0.0)
    hext_scr[rows_a:] = jnp.concatenate([_norm_mod(x_ref[0, rows_a - HALO:], gm, shift), hn],
                                        axis=0).astype(BF16)
    rec_chunk(0, rows_a, n_ext)

    def after_gate_matmul(hd):
        p, q = divmod(hd, HEADS_PER_TILE)
        if q == HEADS_PER_TILE - 1 and p + 1 < len(rec_scrs):
            rec_chunk(p + 1, 0, n_ext)

    def uh_of_head(hd):
        p, q = divmod(hd, HEADS_PER_TILE)
        ln = slice(hd * RG_BLOCK, (hd + 1) * RG_BLOCK)
        uh = half_cb[:, ln] + sum(half_cw[k:k + 1, ln] * _shifted(rec_scrs[p], q, k - RG_CONV // 2, tm)
                                  for k in range(RG_CONV))
        uh_ref[0, :, ln] = uh
        return uh

    _rg_gates_scan(uh_of_head, after_gate_matmul, wg_ref, ba_ref, bx_ref, lam_ref, h0_ref, carry_out_ref,
                   a_scr, b_scr, h_scr, carry_ref, tm=tm, nt=nt, reverse=True)
    for hd in range(RG_HEADS):
        h_ref[0, :, hd * RG_BLOCK:(hd + 1) * RG_BLOCK] = _head_h(h_scr, hd, tm)


def _rg_final_kernel(x_ref, mod_ref, gpre_ref, uh_ref, wg_ref, ba_ref, bx_ref, lam_ref, h0_ref, hb_ref,
                     wgate_ref, wout_ref, gpost_ref, o_ref, carry_out_ref, a_scr, b_scr, h_scr, carry_ref,
                     hm_scr, gz_scr, gy_scr, *, tm, nt):
    shift, scale, gate = mod_ref[0, 0:1, :], mod_ref[0, 1:2, :], mod_ref[0, 2:3, :]
    hm_scr[...] = _norm_mod(x_ref[0], gpre_ref[...] * (1.0 + scale), shift).astype(BF16)

    def after_gate_matmul(hd):
        pass

    def uh_of_head(hd):
        if hd % 2 == 0:
            ln2 = slice(hd * RG_BLOCK, (hd + 2) * RG_BLOCK)
            gz_scr[:, ln2] = jnp.dot(hm_scr[...], wgate_ref[:, ln2], preferred_element_type=F32)
        return uh_ref[0, :, hd * RG_BLOCK:(hd + 1) * RG_BLOCK]

    _rg_gates_scan(uh_of_head, after_gate_matmul, wg_ref, ba_ref, bx_ref, lam_ref, h0_ref, carry_out_ref,
                   a_scr, b_scr, h_scr, carry_ref, tm=tm, nt=nt, reverse=False)
    out = None
    for part in range(OUT_PARTS):
        kk = slice(part * (D_MODEL // OUT_PARTS), (part + 1) * (D_MODEL // OUT_PARTS))
        for hd in range(part * RG_HEADS // OUT_PARTS, (part + 1) * RG_HEADS // OUT_PARTS):
            ln = slice(hd * RG_BLOCK, (hd + 1) * RG_BLOCK)
            gy_scr[:, ln] = (_gelu(gz_scr[:, ln]) * (_head_h(h_scr, hd, tm) + hb_ref[0, :, ln])).astype(BF16)
        partial = jnp.dot(gy_scr[:, kk], wout_ref[kk, :], preferred_element_type=F32)
        out = partial if out is None else out + partial
    o_ref[0] = x_ref[0] + _norm_mod(out, gate * gpost_ref[...], None)


def _scan_scratch(tm):
    rows = SCAN_SEGS * _seg_pitch(tm // SCAN_SEGS)
    return [pltpu.VMEM((RG_HEADS, rows, RG_BLOCK), F32)] * 3 + [pltpu.VMEM((RG_HEADS, 1, RG_BLOCK), F32)]


def _rg_reverse(x, mod, g_pre, w_rec, conv_w, conv_b, w_gates, b_a, b_x, lam, h0, *, tm):
    b, t, d = x.shape
    nt = t // tm
    main, prev, nxt = _tile_specs(tm, nt, True)
    carry_spec = pl.BlockSpec((1, SUBLANES, d), lambda bb, j: (bb, 0, 0))
    return pl.pallas_call(
        functools.partial(_rg_scan_kernel, tm=tm, nt=nt),
        grid=(b, nt),
        in_specs=[main, prev, nxt, _mod_spec(mod.shape[0]), _const_spec((1, d)), _const_spec(w_rec.shape),
                  _const_spec(conv_w.shape), _const_spec((1, d)), _const_spec(w_gates.shape),
                  _const_spec((1, d)), _const_spec((1, d)), _const_spec((1, d)), carry_spec],
        out_specs=[main, main, carry_spec],
        out_shape=[jax.ShapeDtypeStruct(x.shape, F32), jax.ShapeDtypeStruct(x.shape, F32),
                   jax.ShapeDtypeStruct((b, SUBLANES, d), F32)],
        scratch_shapes=[pltpu.VMEM((tm + 2 * HALO, d), BF16),
                        pltpu.VMEM((HEADS_PER_TILE, tm + 2 * HALO, RG_BLOCK), F32),
                        pltpu.VMEM((HEADS_PER_TILE, tm + 2 * HALO, RG_BLOCK), F32)] + _scan_scratch(tm),
        compiler_params=pltpu.CompilerParams(dimension_semantics=("arbitrary", "arbitrary"),
                                             vmem_limit_bytes=VMEM_LIMIT),
        name="rg_scan",
    )(x, x, x, mod, g_pre, w_rec, conv_w, conv_b, w_gates, b_a, b_x, lam, h0)


def _rg_forward(x, mod, g_pre, uh, w_gates, b_a, b_x, lam, h0, h_rev, w_gate, w_out, g_post, *, tm):
    b, t, d = x.shape
    nt = t // tm
    main = pl.BlockSpec((1, tm, d), lambda bb, j: (bb, j, 0))
    carry_spec = pl.BlockSpec((1, SUBLANES, d), lambda bb, j: (bb, 0, 0))
    return pl.pallas_call(
        functools.partial(_rg_final_kernel, tm=tm, nt=nt),
        grid=(b, nt),
        in_specs=[main, _mod_spec(mod.shape[0]), _const_spec((1, d)), main, _const_spec(w_gates.shape),
                  _const_spec((1, d)), _const_spec((1, d)), _const_spec((1, d)), carry_spec, main,
                  _const_spec(w_gate.shape), _const_spec(w_out.shape), _const_spec((1, d))],
        out_specs=[main, carry_spec],
        out_shape=[jax.ShapeDtypeStruct(x.shape, F32), jax.ShapeDtypeStruct((b, SUBLANES, d), F32)],
        scratch_shapes=_scan_scratch(tm) + [pltpu.VMEM((tm, d), BF16), pltpu.VMEM((tm, d), F32),
                                            pltpu.VMEM((tm, d), BF16)],
        compiler_params=pltpu.CompilerParams(dimension_semantics=("arbitrary", "arbitrary"),
                                             vmem_limit_bytes=VMEM_LIMIT),
        name="rg_final",
    )(x, mod, g_pre, uh, w_gates, b_a, b_x, lam, h0, h_rev, w_gate, w_out, g_post)


def _pool_kernel(*refs, tm, nt, width, halo_rows):
    if halo_rows:
        (x_ref, xp_ref, xn_ref, mod_ref, gpre_ref, gpost_ref, band_ref, invcc_ref, rowinv_ref, pw_ref, pb_ref,
         ps_ref, o_ref, hext_scr, rs_scr) = refs
    else:
        (x_ref, mod_ref, gpre_ref, gpost_ref, band_ref, invcc_ref, rowinv_ref, pw_ref, pb_ref, ps_ref,
         o_ref, hext_scr, rs_scr) = refs
    j = pl.program_id(1)
    rt = tm // width
    shift, scale, gate = mod_ref[0, 0:1, :], mod_ref[0, 1:2, :], mod_ref[0, 2:3, :]
    gm = gpre_ref[...] * (1.0 + scale)

    def prep(v):
        return _norm_mod(v, gm, shift)

    base = halo_rows * width
    hext_scr[base:base + tm] = prep(x_ref[0])
    def halo_row(src_ref, i, dst_row, reach, valid):
        groups = [g for g, w in enumerate(POOL_WINDOWS) if w // 2 >= reach]
        if groups:
            lanes = slice(groups[0] * POOL_GROUP, D_MODEL)
            v = src_ref[0, i * width:(i + 1) * width, :]
            r = lax.rsqrt(jnp.mean(v * v, axis=-1, keepdims=True) + EPS)
            y = (v[:, lanes] * r) * gm[:, lanes] + shift[:, lanes]
            hext_scr[dst_row * width:(dst_row + 1) * width, lanes] = jnp.where(valid, y, 0.0)

    for i in range(halo_rows):
        halo_row(xp_ref, i, i, halo_rows - i, j > 0)
        halo_row(xn_ref, i, halo_rows + rt + i, i + 2, j < nt - 1)

    def grid_row(r, ln):
        return hext_scr[(r + halo_rows) * width:(r + halo_rows + 1) * width, ln]

    for g, w in enumerate(POOL_WINDOWS):
        ln = slice(g * POOL_GROUP, (g + 1) * POOL_GROUP)
        if halo_rows:
            win = grid_row(-(w // 2), ln)
            for dr in range(-(w // 2) + 1, w // 2):
                win = win + grid_row(dr, ln)
        else:
            win = grid_row(0, ln)
        for r in range(rt):
            rs_scr[r * width:(r + 1) * width, ln] = win * rowinv_ref[pl.ds(j * rt + r, 1), ln]
            if r + 1 < rt:
                win = win + grid_row(r + w // 2, ln) - grid_row(r - w // 2, ln)

    blocks = [(ch, g) for ch in range(tm // POOL_CHUNK) for g in range(len(POOL_WINDOWS))]

    def window_sums(ch, g):
        ln = slice(g * POOL_GROUP, (g + 1) * POOL_GROUP)
        rs = rs_scr[ch * POOL_CHUNK:(ch + 1) * POOL_CHUNK, ln]
        hi = rs.astype(BF16)
        lo = (rs - hi.astype(F32)).astype(BF16)
        return (jnp.dot(band_ref[g], hi, preferred_element_type=F32)
                + jnp.dot(band_ref[g], lo, preferred_element_type=F32))

    def group_map(ch, g, sums):
        ln = slice(g * POOL_GROUP, (g + 1) * POOL_GROUP)
        hg = hext_scr[base + ch * POOL_CHUNK:base + (ch + 1) * POOL_CHUNK, ln]
        yg = jnp.dot((sums * invcc_ref[:, ln] - hg).astype(BF16), pw_ref[g], preferred_element_type=F32)
        rs_scr[ch * POOL_CHUNK:(ch + 1) * POOL_CHUNK, ln] = (yg + pb_ref[:, ln]) * ps_ref[:, ln]

    sums = window_sums(*blocks[0])
    for k, blk in enumerate(blocks):
        nxt = window_sums(*blocks[k + 1]) if k + 1 < len(blocks) else None
        group_map(*blk, sums)
        sums = nxt

    o_ref[0] = x_ref[0] + _norm_mod(rs_scr[...], gate * gpost_ref[...], None)


def _pool_consts(n_rows, width):
    idx = np.arange(POOL_CHUNK)
    row, col = idx // width, idx % width
    band = np.zeros((len(POOL_WINDOWS), POOL_CHUNK, POOL_CHUNK), np.float32)
    invcc = np.zeros((POOL_CHUNK, D_MODEL), np.float32)
    rowinv = np.zeros((n_rows, D_MODEL), np.float32)
    r = np.arange(n_rows)
    for g, w in enumerate(POOL_WINDOWS):
        same = row[:, None] == row[None, :]
        d = col[None, :] - col[:, None]
        band[g] = (same & (d >= -(w // 2)) & (d < w // 2)).astype(np.float32)
        cc = np.minimum(col + w // 2, width) - np.maximum(col - w // 2, 0)
        invcc[:, g * POOL_GROUP:(g + 1) * POOL_GROUP] = (1.0 / cc)[:, None]
        cr = np.minimum(r + w // 2, n_rows) - np.maximum(r - w // 2, 0)
        rowinv[:, g * POOL_GROUP:(g + 1) * POOL_GROUP] = (1.0 / cr)[:, None]
    return jnp.asarray(band, BF16), jnp.asarray(invcc), jnp.asarray(rowinv)


def _pool(x, mod, g_pre, g_post, pw, pb, ps, *, width, tm):
    b, t, d = x.shape
    n_rows = t // width
    nt = t // tm
    halo_rows = max(POOL_WINDOWS) // 2 if nt > 1 else 0
    assert nt > 1 or n_rows == 1
    band, invcc, rowinv = _pool_consts(n_rows, width)
    main = pl.BlockSpec((1, tm, d), lambda bb, j: (bb, j, 0))
    in_specs, args = [main], [x]
    if halo_rows:
        hb = halo_rows * width
        per = tm // hb
        last = t // hb - 1
        in_specs += [pl.BlockSpec((1, hb, d), lambda bb, j: (bb, jnp.maximum(j * per - 1, 0), 0)),
                     pl.BlockSpec((1, hb, d), lambda bb, j: (bb, jnp.minimum((j + 1) * per, last), 0))]
        args += [x, x]
    in_specs += [_mod_spec(mod.shape[0]), _const_spec((1, d)), _const_spec((1, d)), _const_spec(band.shape),
                 _const_spec(invcc.shape), _const_spec(rowinv.shape), _const_spec(pw.shape),
                 _const_spec((1, d)), _const_spec((1, d))]
    args += [mod, g_pre, g_post, band, invcc, rowinv, pw, pb, ps]
    kern = functools.partial(_pool_kernel, tm=tm, nt=nt, width=width, halo_rows=halo_rows)
    return pl.pallas_call(
        kern,
        grid=(b, nt),
        in_specs=in_specs,
        out_specs=main,
        out_shape=jax.ShapeDtypeStruct(x.shape, F32),
        scratch_shapes=[pltpu.VMEM((tm + 2 * halo_rows * width, d), F32), pltpu.VMEM((tm, d), F32)],
        compiler_params=pltpu.CompilerParams(dimension_semantics=("parallel", "parallel"),
                                             vmem_limit_bytes=VMEM_LIMIT),
        name="pool_mix",
    )(*args)


def kernel(x, c, ctx, c_ctx, ada_w, ada_b, norm_pre_mix, norm_post_mix, norm_pre_ffn, norm_post_ffn, rg_w_in,
           rg_conv_w, rg_conv_b, rg_gate_a_w, rg_gate_a_b, rg_gate_x_w, rg_gate_x_b, rg_lambda, rg_w_out, pool_w,
           pool_b, pool_scale, ffn_w_up, ffn_conv_w, ffn_conv_b, ffn_w_down):
    bsz, seq, d = x.shape
    ctx_len = ctx.shape[1]
    depth = ada_w.shape[0]
    assert d == D_MODEL and seq % GRID_W == 0

    ct = jnp.zeros((d, SUBLANES), F32).at[:, :bsz].set(c.T).at[:, bsz].set(c_ctx)
    mods = _ada(ct, ada_w, ada_b, bsz + 1)
    row = lambda a, i: a[i].reshape(1, -1)
    tm_lat = 512
    zeros_state = jnp.zeros((bsz, SUBLANES, d), F32)

    for i in range(depth):
        last = i == depth - 1
        mod_l = mods[i, :bsz].reshape(bsz, N_MOD, d)
        mod_c = mods[i, bsz].reshape(1, N_MOD, d)
        g_pre, g_post = row(norm_pre_mix, i), row(norm_post_mix, i)
        jm = i // 2
        if i % 2 == 0:
            w_gate = rg_w_in[jm, :, :d].astype(BF16)
            w_rec = rg_w_in[jm, :, d:].astype(BF16)
            w_out = rg_w_out[jm].astype(BF16)
            conv_w, conv_b = rg_conv_w[jm], row(rg_conv_b, jm)
            dirs = []
            for dd in range(2):
                wg = jnp.concatenate([rg_gate_a_w[jm, dd], rg_gate_x_w[jm, dd]], axis=-1)
                zg = jnp.zeros_like(wg[0::2])
                wg = jnp.concatenate([jnp.concatenate([wg[0::2], zg], axis=-1),
                                      jnp.concatenate([zg, wg[1::2]], axis=-1)], axis=1).astype(BF16)
                dirs.append((wg, rg_gate_a_b[jm, dd].reshape(1, d), rg_gate_x_b[jm, dd].reshape(1, d),
                             rg_lambda[jm, dd].reshape(1, d)))
            hb_c, uh_c, end_b = _rg_reverse(ctx, mod_c, g_pre, w_rec, conv_w, conv_b, *dirs[1], zeros_state,
                                            tm=ctx_len)
            ctx_new, end_f = _rg_forward(ctx, mod_c, g_pre, uh_c, *dirs[0], zeros_state, hb_c, w_gate, w_out,
                                         g_post, tm=ctx_len)
            hb_l, uh_l, _ = _rg_reverse(x, mod_l, g_pre, w_rec, conv_w, conv_b, *dirs[1], end_b, tm=1024)
            x, _ = _rg_forward(x, mod_l, g_pre, uh_l, *dirs[0], end_f, hb_l, w_gate, w_out, g_post, tm=tm_lat)
            if not last:
                ctx = ctx_new
        else:
            pw = pool_w[jm].astype(BF16)
            pb, ps = row(pool_b, jm), row(pool_scale, jm)
            x_new = _pool(x, mod_l, g_pre, g_post, pw, pb, ps, width=GRID_W, tm=1024)
            if not last:
                ctx = _pool(ctx, mod_c, g_pre, g_post, pw, pb, ps, width=ctx_len, tm=ctx_len)
            x = x_new

        g_pre, g_post = row(norm_pre_ffn, i), row(norm_post_ffn, i)
        w_up, w_dn = ffn_w_up[i].astype(BF16), ffn_w_down[i].astype(BF16)
        cw, cb = ffn_conv_w[i], row(ffn_conv_b, i)
        x = _ffn(x, mod_l, g_pre, g_post, w_up, cw, cb, w_dn, tm=1024)
        if not last:
            ctx = _ffn(ctx, mod_c, g_pre, g_post, w_up, cw, cb, w_dn, tm=ctx_len)
    return x
```

```python
import functools
import math

import numpy as np
import jax
import jax.numpy as jnp
from jax import lax
from jax.experimental import pallas as pl
from jax.experimental.pallas import tpu as pltpu

D_MODEL = 1024
GRID_W = 64
N_MOD = 6
RG_HEADS = 8
RG_BLOCK = D_MODEL // RG_HEADS
RG_CONV = 4
RG_C = 8.0
POOL_WINDOWS = (2, 4, 8, 16)
POOL_GROUP = D_MODEL // len(POOL_WINDOWS)
D_FF = 3 * D_MODEL
EPS = 1e-6
TINY = 1e-30

SUBLANES = 8
LANES = 128
BF16_ROWS = 16
MXU_N = 512
HEADS_PER_TILE = MXU_N // RG_BLOCK
HALO = 8
POOL_CHUNK = 256
OUT_PARTS = 4
SCAN_SEGS = 16
VMEM_LIMIT = 60 * 1024 * 1024

F32 = jnp.float32
BF16 = jnp.bfloat16


def _norm_mod(v, gm, shift):
    r = lax.rsqrt(jnp.mean(v * v, axis=-1, keepdims=True) + EPS)
    y = (v * r) * gm
    return y if shift is None else y + shift


def _gelu(x):
    k1 = math.sqrt(2.0 / math.pi)
    hx = 0.5 * x
    return hx + hx * jnp.tanh(x * (k1 + (k1 * 0.044715) * (x * x)))


def _sigmoid(x):
    return 1.0 / (1.0 + jnp.exp(-x))


def _const_spec(shape):
    nd = len(shape)
    return pl.BlockSpec(shape, lambda *_: (0,) * nd, pipeline_mode=pl.Buffered(1))


def _tile_specs(tm, nt, reverse):
    per = tm // HALO
    last = nt * per - 1

    def tile(j):
        return (nt - 1 - j) if reverse else j

    main = pl.BlockSpec((1, tm, D_MODEL), lambda b, j: (b, tile(j), 0))
    prev = pl.BlockSpec((1, HALO, D_MODEL), lambda b, j: (b, jnp.maximum(tile(j) * per - 1, 0), 0))
    nxt = pl.BlockSpec((1, HALO, D_MODEL), lambda b, j: (b, jnp.minimum((tile(j) + 1) * per, last), 0))
    return main, prev, nxt


def _mod_spec(n_mod_rows):
    if n_mod_rows == 1:
        return pl.BlockSpec((1, N_MOD, D_MODEL), lambda b, j: (0, 0, 0))
    return pl.BlockSpec((1, N_MOD, D_MODEL), lambda b, j: (b, 0, 0))


def _ada_kernel(ct_ref, w_ref, b_ref, o_ref, *, n_rows):
    ct = ct_ref[...]
    st = ct * _sigmoid(ct)
    w = w_ref[0]
    rows = [jnp.sum(w * st[:, m:m + 1], axis=0, keepdims=True) for m in range(n_rows)]
    rows += [jnp.zeros_like(rows[0])] * (SUBLANES - n_rows)
    o_ref[0] = jnp.concatenate(rows, axis=0) + b_ref[0]


def _ada(ct, ada_w, ada_b, n_rows):
    depth, d, n = ada_w.shape
    tn = 1536
    return pl.pallas_call(
        functools.partial(_ada_kernel, n_rows=n_rows),
        grid=(depth, n // tn),
        in_specs=[pl.BlockSpec((d, SUBLANES), lambda i, k: (0, 0)),
                  pl.BlockSpec((1, d, tn), lambda i, k: (i, 0, k)),
                  pl.BlockSpec((1, 1, tn), lambda i, k: (i, 0, k))],
        out_specs=pl.BlockSpec((1, SUBLANES, tn), lambda i, k: (i, 0, k)),
        out_shape=jax.ShapeDtypeStruct((depth, SUBLANES, n), F32),
        compiler_params=pltpu.CompilerParams(dimension_semantics=("parallel", "parallel"),
                                             vmem_limit_bytes=VMEM_LIMIT),
        name="ada_mod",
    )(ct, ada_w, ada_b.reshape(depth, 1, n))


def _prep_ext(x_ref, xp_ref, xn_ref, g_pre, shift, scale, tile, nt):
    gm = g_pre * (1.0 + scale)

    def prep(v):
        return _norm_mod(v, gm, shift)

    hp = jnp.where(tile > 0, prep(xp_ref[0]), 0.0)
    hn = jnp.where(tile < nt - 1, prep(xn_ref[0]), 0.0)
    return jnp.concatenate([hp, prep(x_ref[0]), hn], axis=0)


def _shifted(slab_ref, slab, k, tm):
    if k == 0:
        return slab_ref[slab, pl.ds(HALO, tm), :]
    return slab_ref[slab, pl.ds(HALO + k, tm, stride=1), :]


def _ffn_kernel(x_ref, xp_ref, xn_ref, mod_ref, gpre_ref, gpost_ref, wup_ref, cw_ref, cb_ref, wdn_ref,
                o_ref, hext_ref, zs0_ref, zs1_ref, zv0_ref, zv1_ref, act_ref, *, tm, nt, fc):
    j = pl.program_id(1)
    shift, scale, gate = mod_ref[0, 3:4, :], mod_ref[0, 4:5, :], mod_ref[0, 5:6, :]
    hext = _prep_ext(x_ref, xp_ref, xn_ref, gpre_ref[...], shift, scale, j, nt)
    hext_ref[...] = hext.astype(BF16)
    ns = fc // LANES
    n_chunks = D_FF // fc
    zs_refs, zv_refs = (zs0_ref, zs1_ref), (zv0_ref, zv1_ref)

    def up(c):
        zs_ref = zs_refs[c % 2]
        z = jnp.dot(hext_ref[...], wup_ref[:, 2 * c * fc:2 * (c + 1) * fc], preferred_element_type=F32)
        for l in range(ns):
            zs_ref[l] = z[:, l * LANES:(l + 1) * LANES]
        zv_refs[c % 2][...] = z[HALO:HALO + tm, fc:]

    def act(c):
        lo, zs_ref, zv_ref = c * fc, zs_refs[c % 2], zv_refs[c % 2]
        for l in range(ns):
            ln = slice(lo + l * LANES, lo + (l + 1) * LANES)
            cw = cw_ref[:, ln]
            u = (cw[0:1] * _shifted(zs_ref, l, -1, tm) + cw[1:2] * _shifted(zs_ref, l, 0, tm)
                 + cw[2:3] * _shifted(zs_ref, l, 1, tm) + cb_ref[:, ln])
            act_ref[:, ln] = (_gelu(u) * zv_ref[:, l * LANES:(l + 1) * LANES]).astype(BF16)

    up(0)
    for c in range(n_chunks):
        if c + 1 < n_chunks:
            up(c + 1)
        act(c)
    out = jnp.dot(act_ref[...], wdn_ref[...], preferred_element_type=F32)
    o_ref[0] = x_ref[0] + _norm_mod(out, gate * gpost_ref[...], None)


FFN_CHUNK = 1024


def _pack_up_weights(w_up, fc):
    d = w_up.shape[0]
    u = w_up[:, :D_FF].reshape(d, D_FF // fc, fc)
    v = w_up[:, D_FF:].reshape(d, D_FF // fc, fc)
    return jnp.concatenate([u, v], axis=2).reshape(d, 2 * D_FF).astype(BF16)


def _ffn(x, mod, g_pre, g_post, w_up, conv_w, conv_b, w_down, *, tm, fc=FFN_CHUNK):
    b, t, d = x.shape
    nt = t // tm
    main, prev, nxt = _tile_specs(tm, nt, False)
    kern = functools.partial(_ffn_kernel, tm=tm, nt=nt, fc=fc)
    assert w_up.shape == (d, 2 * D_FF)
    return pl.pallas_call(
        kern,
        grid=(b, nt),
        in_specs=[main, prev, nxt, _mod_spec(mod.shape[0]),
                  _const_spec((1, d)), _const_spec((1, d)),
                  _const_spec(w_up.shape), _const_spec(conv_w.shape), _const_spec((1, D_FF)),
                  _const_spec(w_down.shape)],
        out_specs=pl.BlockSpec((1, tm, d), lambda bb, j: (bb, j, 0)),
        out_shape=jax.ShapeDtypeStruct(x.shape, F32),
        scratch_shapes=[pltpu.VMEM((tm + 2 * HALO, d), BF16),
                        pltpu.VMEM((fc // LANES, tm + 2 * HALO, LANES), F32),
                        pltpu.VMEM((fc // LANES, tm + 2 * HALO, LANES), F32),
                        pltpu.VMEM((tm, fc), F32), pltpu.VMEM((tm, fc), F32), pltpu.VMEM((tm, D_FF), BF16)],
        compiler_params=pltpu.CompilerParams(dimension_semantics=("parallel", "parallel"),
                                             vmem_limit_bytes=VMEM_LIMIT),
        name="conv_ffn",
    )(x, x, x, mod, g_pre, g_post, w_up, conv_w, conv_b, w_down)


def _seg_pitch(seg):
    p = seg + 4
    assert p % 8 == 4, seg
    return p


def _scan_tile(a_scr, b_scr, h_scr, carry_ref, *, seg, reverse):
    pitch = _seg_pitch(seg)
    nh = a_scr.shape[0]
    nv = SCAN_SEGS // SUBLANES
    sub = lax.broadcasted_iota(jnp.int32, (SUBLANES, RG_BLOCK), 0)
    chains = [(hd, v) for hd in range(nh) for v in range(nv)]

    def rows(v, i):
        return pl.ds(v * SUBLANES * pitch + ((seg - 1 - i) if reverse else i), SUBLANES, stride=pitch)

    def pass1(i, st):
        out = []
        for k, (hd, v) in enumerate(chains):
            av = a_scr[hd, rows(v, i), :]
            out += [av * st[2 * k] + b_scr[hd, rows(v, i), :], st[2 * k + 1] * av]
        return tuple(out)

    init = tuple(jnp.zeros((SUBLANES, RG_BLOCK), F32) if k % 2 == 0 else jnp.ones((SUBLANES, RG_BLOCK), F32)
                 for k in range(2 * len(chains)))
    st = init
    for i in range(seg):
        st = pass1(i, st)

    first = SUBLANES - 1 if reverse else 0
    last = 0 if reverse else SUBLANES - 1
    cins = {}
    for hd in range(nh):
        carry = jnp.broadcast_to(carry_ref[hd], (SUBLANES, RG_BLOCK))
        for v in (reversed(range(nv)) if reverse else range(nv)):
            k = hd * nv + v
            e, pe = st[2 * k], st[2 * k + 1]
            cin = carry
            for _ in range(SUBLANES - 1):
                nxt = pltpu.roll(e + pe * cin, (SUBLANES - 1) if reverse else 1, axis=0)
                cin = jnp.where(sub == first, carry, nxt)
            cins[(hd, v)] = cin
            end = e + pe * cin
            carry = jnp.broadcast_to(end[last:last + 1, :], (SUBLANES, RG_BLOCK))
        carry_ref[hd] = carry[0:1, :]

    def pass2(i, hs):
        out = []
        for k, (hd, v) in enumerate(chains):
            h = a_scr[hd, rows(v, i), :] * hs[k] + b_scr[hd, rows(v, i), :]
            h_scr[hd, rows(v, i), :] = h
            out.append(h)
        return tuple(out)

    hs = tuple(cins[ch] for ch in chains)
    for i in range(seg):
        hs = pass2(i, hs)


def _rg_gates_scan(uh_of_head, after_gate_matmul, wg_ref, ba_ref, bx_ref, lam_ref, h0_ref, carry_out_ref,
                   a_scr, b_scr, h_scr, carry_ref, *, tm, nt, reverse):
    j = pl.program_id(1)
    seg = tm // SCAN_SEGS
    pitch = _seg_pitch(seg)

    @pl.when(j == 0)
    def _():
        for hd in range(RG_HEADS):
            carry_ref[hd] = h0_ref[0, 0:1, hd * RG_BLOCK:(hd + 1) * RG_BLOCK]

    lam = lam_ref[...]
    sp = jnp.maximum(-lam, 0.0) + jnp.log(1.0 + jnp.exp(-jnp.abs(lam)))
    nc = (-0.5 * RG_C * math.log2(math.e)) * sp
    half_ba, half_bx = 0.5 * ba_ref[...], 0.5 * bx_ref[...]
    def gate_matmul(pair):
        uhs = [uh_of_head(2 * pair + q) for q in range(2)]
        gz = jnp.dot(jnp.concatenate(uhs, axis=1).astype(BF16), wg_ref[pair], preferred_element_type=F32)
        after_gate_matmul(2 * pair + 1)
        return uhs, gz

    n_pairs = RG_HEADS // 2
    queued = gate_matmul(0)
    for pair in range(n_pairs):
        uhs, gz = queued
        if pair + 1 < n_pairs:
            queued = gate_matmul(pair + 1)
        for q in range(2):
            hd, uh = 2 * pair + q, uhs[q]
            ln = slice(hd * RG_BLOCK, (hd + 1) * RG_BLOCK)
            tr = jnp.tanh(gz[:, 2 * q * RG_BLOCK:(2 * q + 1) * RG_BLOCK] + half_ba[:, ln])
            ti = jnp.tanh(gz[:, (2 * q + 1) * RG_BLOCK:(2 * q + 2) * RG_BLOCK] + half_bx[:, ln])
            a = jnp.exp2(nc[:, ln] + nc[:, ln] * tr)
            y1 = 1.0 - a * a
            mult = y1 * lax.rsqrt(jnp.maximum(y1, TINY))
            bb = mult * (uh + uh * ti)
            for s in range(SCAN_SEGS):
                a_scr[hd, pl.ds(s * pitch, seg, stride=1), :] = a[s * seg:(s + 1) * seg]
                b_scr[hd, pl.ds(s * pitch, seg, stride=1), :] = bb[s * seg:(s + 1) * seg]

    _scan_tile(a_scr, b_scr, h_scr, carry_ref, seg=seg, reverse=reverse)

    @pl.when(j == nt - 1)
    def _():
        for hd in range(RG_HEADS):
            carry_out_ref[0, :, hd * RG_BLOCK:(hd + 1) * RG_BLOCK] = jnp.broadcast_to(
                carry_ref[hd], (SUBLANES, RG_BLOCK))


def _head_h(h_scr, hd, tm):
    seg = tm // SCAN_SEGS
    pitch = _seg_pitch(seg)
    return jnp.concatenate([h_scr[hd, pl.ds(s * pitch, seg, stride=1), :] for s in range(SCAN_SEGS)], axis=0)


def _rg_scan_kernel(x_ref, xp_ref, xn_ref, mod_ref, gpre_ref, wr_ref, cw_ref, cb_ref, wg_ref, ba_ref, bx_ref,
                    lam_ref, h0_ref, h_ref, uh_ref, carry_out_ref, hext_scr, rec0_scr, rec1_scr, a_scr, b_scr,
                    h_scr, carry_ref, *, tm, nt):
    tile = nt - 1 - pl.program_id(1)
    shift, scale = mod_ref[0, 0:1, :], mod_ref[0, 1:2, :]
    gm = gpre_ref[...] * (1.0 + scale)
    half_cw, half_cb = 0.5 * cw_ref[...], 0.5 * cb_ref[...]

    rec_scrs = (rec0_scr, rec1_scr)
    assert len(rec_scrs) * HEADS_PER_TILE == RG_HEADS
    n_ext = tm + 2 * HALO
    rows_a = -(-(HALO + tm // 2) // BF16_ROWS) * BF16_ROWS

    def rec_chunk(p, lo, hi):
        r = jnp.dot(hext_scr[lo:hi], wr_ref[:, p * MXU_N:(p + 1) * MXU_N], preferred_element_type=F32)
        for q in range(HEADS_PER_TILE):
            rec_scrs[p][q, lo:hi] = r[:, q * RG_BLOCK:(q + 1) * RG_BLOCK]

    hp = jnp.where(tile > 0, _norm_mod(xp_ref[0], gm, shift), 0.0)
    hext_scr[:rows_a] = jnp.concatenate([hp, _norm_mod(x_ref[0, :rows_a - HALO], gm, shift)],
                                        axis=0).astype(BF16)
    rec_chunk(0, 0, rows_a)
    hn = jnp.where(tile < nt - 1, _norm_mod(xn_ref[0], gm, shift), 0.0)
    hext_scr[rows_a:] = jnp.concatenate([_norm_mod(x_ref[0, rows_a - HALO:], gm, shift), hn],
                                        axis=0).astype(BF16)
    rec_chunk(0, rows_a, n_ext)

    def after_gate_matmul(hd):
        p, q = divmod(hd, HEADS_PER_TILE)
        if q == HEADS_PER_TILE - 1 and p + 1 < len(rec_scrs):
            rec_chunk(p + 1, 0, n_ext)

    def uh_of_head(hd):
        p, q = divmod(hd, HEADS_PER_TILE)
        ln = slice(hd * RG_BLOCK, (hd + 1) * RG_BLOCK)
        uh = half_cb[:, ln] + sum(half_cw[k:k + 1, ln] * _shifted(rec_scrs[p], q, k - RG_CONV // 2, tm)
                                  for k in range(RG_CONV))
        uh_ref[0, :, ln] = uh
        return uh

    _rg_gates_scan(uh_of_head, after_gate_matmul, wg_ref, ba_ref, bx_ref, lam_ref, h0_ref, carry_out_ref,
                   a_scr, b_scr, h_scr, carry_ref, tm=tm, nt=nt, reverse=True)
    for hd in range(RG_HEADS):
        h_ref[0, :, hd * RG_BLOCK:(hd + 1) * RG_BLOCK] = _head_h(h_scr, hd, tm)


def _rg_final_kernel(x_ref, mod_ref, gpre_ref, uh_ref, wg_ref, ba_ref, bx_ref, lam_ref, h0_ref, hb_ref,
                     wgate_ref, wout_ref, gpost_ref, o_ref, carry_out_ref, a_scr, b_scr, h_scr, carry_ref,
                     hm_scr, gz_scr, gy_scr, *, tm, nt):
    shift, scale, gate = mod_ref[0, 0:1, :], mod_ref[0, 1:2, :], mod_ref[0, 2:3, :]
    hm_scr[...] = _norm_mod(x_ref[0], gpre_ref[...] * (1.0 + scale), shift).astype(BF16)

    def after_gate_matmul(hd):
        pass

    def uh_of_head(hd):
        if hd % 2 == 0:
            ln2 = slice(hd * RG_BLOCK, (hd + 2) * RG_BLOCK)
            gz_scr[:, ln2] = jnp.dot(hm_scr[...], wgate_ref[:, ln2], preferred_element_type=F32)
        return uh_ref[0, :, hd * RG_BLOCK:(hd + 1) * RG_BLOCK]

    _rg_gates_scan(uh_of_head, after_gate_matmul, wg_ref, ba_ref, bx_ref, lam_ref, h0_ref, carry_out_ref,
                   a_scr, b_scr, h_scr, carry_ref, tm=tm, nt=nt, reverse=False)
    out = None
    for part in range(OUT_PARTS):
        kk = slice(part * (D_MODEL // OUT_PARTS), (part + 1) * (D_MODEL // OUT_PARTS))
        for hd in range(part * RG_HEADS // OUT_PARTS, (part + 1) * RG_HEADS // OUT_PARTS):
            ln = slice(hd * RG_BLOCK, (hd + 1) * RG_BLOCK)
            gy_scr[:, ln] = (_gelu(gz_scr[:, ln]) * (_head_h(h_scr, hd, tm) + hb_ref[0, :, ln])).astype(BF16)
        partial = jnp.dot(gy_scr[:, kk], wout_ref[kk, :], preferred_element_type=F32)
        out = partial if out is None else out + partial
    o_ref[0] = x_ref[0] + _norm_mod(out, gate * gpost_ref[...], None)


def _scan_scratch(tm):
    rows = SCAN_SEGS * _seg_pitch(tm // SCAN_SEGS)
    return [pltpu.VMEM((RG_HEADS, rows, RG_BLOCK), F32)] * 3 + [pltpu.VMEM((RG_HEADS, 1, RG_BLOCK), F32)]


def _rg_reverse(x, mod, g_pre, w_rec, conv_w, conv_b, w_gates, b_a, b_x, lam, h0, *, tm):
    b, t, d = x.shape
    nt = t // tm
    main, prev, nxt = _tile_specs(tm, nt, True)
    carry_spec = pl.BlockSpec((1, SUBLANES, d), lambda bb, j: (bb, 0, 0))
    return pl.pallas_call(
        functools.partial(_rg_scan_kernel, tm=tm, nt=nt),
        grid=(b, nt),
        in_specs=[main, prev, nxt, _mod_spec(mod.shape[0]), _const_spec((1, d)), _const_spec(w_rec.shape),
                  _const_spec(conv_w.shape), _const_spec((1, d)), _const_spec(w_gates.shape),
                  _const_spec((1, d)), _const_spec((1, d)), _const_spec((1, d)), carry_spec],
        out_specs=[main, main, carry_spec],
        out_shape=[jax.ShapeDtypeStruct(x.shape, F32), jax.ShapeDtypeStruct(x.shape, F32),
                   jax.ShapeDtypeStruct((b, SUBLANES, d), F32)],
        scratch_shapes=[pltpu.VMEM((tm + 2 * HALO, d), BF16),
                        pltpu.VMEM((HEADS_PER_TILE, tm + 2 * HALO, RG_BLOCK), F32),
                        pltpu.VMEM((HEADS_PER_TILE, tm + 2 * HALO, RG_BLOCK), F32)] + _scan_scratch(tm),
        compiler_params=pltpu.CompilerParams(dimension_semantics=("arbitrary", "arbitrary"),
                                             vmem_limit_bytes=VMEM_LIMIT),
        name="rg_scan",
    )(x, x, x, mod, g_pre, w_rec, conv_w, conv_b, w_gates, b_a, b_x, lam, h0)


def _rg_forward(x, mod, g_pre, uh, w_gates, b_a, b_x, lam, h0, h_rev, w_gate, w_out, g_post, *, tm):
    b, t, d = x.shape
    nt = t // tm
    main = pl.BlockSpec((1, tm, d), lambda bb, j: (bb, j, 0))
    carry_spec = pl.BlockSpec((1, SUBLANES, d), lambda bb, j: (bb, 0, 0))
    return pl.pallas_call(
        functools.partial(_rg_final_kernel, tm=tm, nt=nt),
        grid=(b, nt),
        in_specs=[main, _mod_spec(mod.shape[0]), _const_spec((1, d)), main, _const_spec(w_gates.shape),
                  _const_spec((1, d)), _const_spec((1, d)), _const_spec((1, d)), carry_spec, main,
                  _const_spec(w_gate.shape), _const_spec(w_out.shape), _const_spec((1, d))],
        out_specs=[main, carry_spec],
        out_shape=[jax.ShapeDtypeStruct(x.shape, F32), jax.ShapeDtypeStruct((b, SUBLANES, d), F32)],
        scratch_shapes=_scan_scratch(tm) + [pltpu.VMEM((tm, d), BF16), pltpu.VMEM((tm, d), F32),
                                            pltpu.VMEM((tm, d), BF16)],
        compiler_params=pltpu.CompilerParams(dimension_semantics=("arbitrary", "arbitrary"),
                                             vmem_limit_bytes=VMEM_LIMIT),
        name="rg_final",
    )(x, mod, g_pre, uh, w_gates, b_a, b_x, lam, h0, h_rev, w_gate, w_out, g_post)


def _pool_kernel(*refs, tm, nt, width, halo_rows):
    if halo_rows:
        (x_ref, xp_ref, xn_ref, mod_ref, gpre_ref, gpost_ref, band_ref, invcc_ref, rowinv_ref, pw_ref, pb_ref,
         ps_ref, o_ref, hext_scr, rs_scr) = refs
    else:
        (x_ref, mod_ref, gpre_ref, gpost_ref, band_ref, invcc_ref, rowinv_ref, pw_ref, pb_ref, ps_ref,
         o_ref, hext_scr, rs_scr) = refs
    j = pl.program_id(1)
    rt = tm // width
    shift, scale, gate = mod_ref[0, 0:1, :], mod_ref[0, 1:2, :], mod_ref[0, 2:3, :]
    gm = gpre_ref[...] * (1.0 + scale)

    def prep(v):
        return _norm_mod(v, gm, shift)

    base = halo_rows * width
    hext_scr[base:base + tm] = prep(x_ref[0])
    def halo_row(src_ref, i, dst_row, reach, valid):
        groups = [g for g, w in enumerate(POOL_WINDOWS) if w // 2 >= reach]
        if groups:
            lanes = slice(groups[0] * POOL_GROUP, D_MODEL)
            v = src_ref[0, i * width:(i + 1) * width, :]
            r = lax.rsqrt(jnp.mean(v * v, axis=-1, keepdims=True) + EPS)
            y = (v[:, lanes] * r) * gm[:, lanes] + shift[:, lanes]
            hext_scr[dst_row * width:(dst_row + 1) * width, lanes] = jnp.where(valid, y, 0.0)

    for i in range(halo_rows):
        halo_row(xp_ref, i, i, halo_rows - i, j > 0)
        halo_row(xn_ref, i, halo_rows + rt + i, i + 2, j < nt - 1)

    def grid_row(r, ln):
        return hext_scr[(r + halo_rows) * width:(r + halo_rows + 1) * width, ln]

    for g, w in enumerate(POOL_WINDOWS):
        ln = slice(g * POOL_GROUP, (g + 1) * POOL_GROUP)
        if halo_rows:
            win = grid_row(-(w // 2), ln)
            for dr in range(-(w // 2) + 1, w // 2):
                win = win + grid_row(dr, ln)
        else:
            win = grid_row(0, ln)
        for r in range(rt):
            rs_scr[r * width:(r + 1) * width, ln] = win * rowinv_ref[pl.ds(j * rt + r, 1), ln]
            if r + 1 < rt:
                win = win + grid_row(r + w // 2, ln) - grid_row(r - w // 2, ln)

    blocks = [(ch, g) for ch in range(tm // POOL_CHUNK) for g in range(len(POOL_WINDOWS))]

    def window_sums(ch, g):
        ln = slice(g * POOL_GROUP, (g + 1) * POOL_GROUP)
        rs = rs_scr[ch * POOL_CHUNK:(ch + 1) * POOL_CHUNK, ln]
        hi = rs.astype(BF16)
        lo = (rs - hi.astype(F32)).astype(BF16)
        return (jnp.dot(band_ref[g], hi, preferred_element_type=F32)
                + jnp.dot(band_ref[g], lo, preferred_element_type=F32))

    def group_map(ch, g, sums):
        ln = slice(g * POOL_GROUP, (g + 1) * POOL_GROUP)
        hg = hext_scr[base + ch * POOL_CHUNK:base + (ch + 1) * POOL_CHUNK, ln]
        yg = jnp.dot((sums * invcc_ref[:, ln] - hg).astype(BF16), pw_ref[g], preferred_element_type=F32)
        rs_scr[ch * POOL_CHUNK:(ch + 1) * POOL_CHUNK, ln] = (yg + pb_ref[:, ln]) * ps_ref[:, ln]

    sums = window_sums(*blocks[0])
    for k, blk in enumerate(blocks):
        nxt = window_sums(*blocks[k + 1]) if k + 1 < len(blocks) else None
        group_map(*blk, sums)
        sums = nxt

    o_ref[0] = x_ref[0] + _norm_mod(rs_scr[...], gate * gpost_ref[...], None)


def _pool_consts(n_rows, width):
    idx = np.arange(POOL_CHUNK)
    row, col = idx // width, idx % width
    band = np.zeros((len(POOL_WINDOWS), POOL_CHUNK, POOL_CHUNK), np.float32)
    invcc = np.zeros((POOL_CHUNK, D_MODEL), np.float32)
    rowinv = np.zeros((n_rows, D_MODEL), np.float32)
    r = np.arange(n_rows)
    for g, w in enumerate(POOL_WINDOWS):
        same = row[:, None] == row[None, :]
        d = col[None, :] - col[:, None]
        band[g] = (same & (d >= -(w // 2)) & (d < w // 2)).astype(np.float32)
        cc = np.minimum(col + w // 2, width) - np.maximum(col - w // 2, 0)
        invcc[:, g * POOL_GROUP:(g + 1) * POOL_GROUP] = (1.0 / cc)[:, None]
        cr = np.minimum(r + w // 2, n_rows) - np.maximum(r - w // 2, 0)
        rowinv[:, g * POOL_GROUP:(g + 1) * POOL_GROUP] = (1.0 / cr)[:, None]
    return jnp.asarray(band, BF16), jnp.asarray(invcc), jnp.asarray(rowinv)


def _pool(x, mod, g_pre, g_post, pw, pb, ps, *, width, tm):
    b, t, d = x.shape
    n_rows = t // width
    nt = t // tm
    halo_rows = max(POOL_WINDOWS) // 2 if nt > 1 else 0
    assert nt > 1 or n_rows == 1
    band, invcc, rowinv = _pool_consts(n_rows, width)
    main = pl.BlockSpec((1, tm, d), lambda bb, j: (bb, j, 0))
    in_specs, args = [main], [x]
    if halo_rows:
        hb = halo_rows * width
        per = tm // hb
        last = t // hb - 1
        in_specs += [pl.BlockSpec((1, hb, d), lambda bb, j: (bb, jnp.maximum(j * per - 1, 0), 0)),
                     pl.BlockSpec((1, hb, d), lambda bb, j: (bb, jnp.minimum((j + 1) * per, last), 0))]
        args += [x, x]
    in_specs += [_mod_spec(mod.shape[0]), _const_spec((1, d)), _const_spec((1, d)), _const_spec(band.shape),
                 _const_spec(invcc.shape), _const_spec(rowinv.shape), _const_spec(pw.shape),
                 _const_spec((1, d)), _const_spec((1, d))]
    args += [mod, g_pre, g_post, band, invcc, rowinv, pw, pb, ps]
    kern = functools.partial(_pool_kernel, tm=tm, nt=nt, width=width, halo_rows=halo_rows)
    return pl.pallas_call(
        kern,
        grid=(b, nt),
        in_specs=in_specs,
        out_specs=main,
        out_shape=jax.ShapeDtypeStruct(x.shape, F32),
        scratch_shapes=[pltpu.VMEM((tm + 2 * halo_rows * width, d), F32), pltpu.VMEM((tm, d), F32)],
        compiler_params=pltpu.CompilerParams(dimension_semantics=("parallel", "parallel"),
                                             vmem_limit_bytes=VMEM_LIMIT),
        name="pool_mix",
    )(*args)


def kernel(x, c, ctx, c_ctx, ada_w, ada_b, norm_pre_mix, norm_post_mix, norm_pre_ffn, norm_post_ffn, rg_w_in,
           rg_conv_w, rg_conv_b, rg_gate_a_w, rg_gate_a_b, rg_gate_x_w, rg_gate_x_b, rg_lambda, rg_w_out, pool_w,
           pool_b, pool_scale, ffn_w_up, ffn_conv_w, ffn_conv_b, ffn_w_down):
    bsz, seq, d = x.shape
    ctx_len = ctx.shape[1]
    depth = ada_w.shape[0]
    assert d == D_MODEL and seq % GRID_W == 0

    ct = jnp.zeros((d, SUBLANES), F32).at[:, :bsz].set(c.T).at[:, bsz].set(c_ctx)
    mods = _ada(ct, ada_w, ada_b, bsz + 1)
    row = lambda a, i: a[i].reshape(1, -1)
    tm_lat = 512
    zeros_state = jnp.zeros((bsz, SUBLANES, d), F32)

    for i in range(depth):
        last = i == depth - 1
        mod_l = mods[i, :bsz].reshape(bsz, N_MOD, d)
        mod_c = mods[i, bsz].reshape(1, N_MOD, d)
        g_pre, g_post = row(norm_pre_mix, i), row(norm_post_mix, i)
        jm = i // 2
        if i % 2 == 0:
            w_gate = rg_w_in[jm, :, :d].astype(BF16)
            w_rec = rg_w_in[jm, :, d:].astype(BF16)
            w_out = rg_w_out[jm].astype(BF16)
            conv_w, conv_b = rg_conv_w[jm], row(rg_conv_b, jm)
            dirs = []
            for dd in range(2):
                wg = jnp.concatenate([rg_gate_a_w[jm, dd], rg_gate_x_w[jm, dd]], axis=-1)
                zg = jnp.zeros_like(wg[0::2])
                wg = jnp.concatenate([jnp.concatenate([wg[0::2], zg], axis=-1),
                                      jnp.concatenate([zg, wg[1::2]], axis=-1)], axis=1).astype(BF16)
                dirs.append((wg, rg_gate_a_b[jm, dd].reshape(1, d), rg_gate_x_b[jm, dd].reshape(1, d),
                             rg_lambda[jm, dd].reshape(1, d)))
            hb_c, uh_c, end_b = _rg_reverse(ctx, mod_c, g_pre, w_rec, conv_w, conv_b, *dirs[1], zeros_state,
                                            tm=ctx_len)
            ctx_new, end_f = _rg_forward(ctx, mod_c, g_pre, uh_c, *dirs[0], zeros_state, hb_c, w_gate, w_out,
                                         g_post, tm=ctx_len)
            hb_l, uh_l, _ = _rg_reverse(x, mod_l, g_pre, w_rec, conv_w, conv_b, *dirs[1], end_b, tm=1024)
            x, _ = _rg_forward(x, mod_l, g_pre, uh_l, *dirs[0], end_f, hb_l, w_gate, w_out, g_post, tm=tm_lat)
            if not last:
                ctx = ctx_new
        else:
            pw = pool_w[jm].astype(BF16)
            pb, ps = row(pool_b, jm), row(pool_scale, jm)
            x_new = _pool(x, mod_l, g_pre, g_post, pw, pb, ps, width=GRID_W, tm=1024)
            if not last:
                ctx = _pool(ctx, mod_c, g_pre, g_post, pw, pb, ps, width=ctx_len, tm=ctx_len)
            x = x_new

        g_pre, g_post = row(norm_pre_ffn, i), row(norm_post_ffn, i)
        w_up, w_dn = _pack_up_weights(ffn_w_up[i], FFN_CHUNK), ffn_w_down[i].astype(BF16)
        cw, cb = ffn_conv_w[i], row(ffn_conv_b, i)
        x = _ffn(x, mod_l, g_pre, g_post, w_up, cw, cb, w_dn, tm=1024)
        if not last:
            ctx = _ffn(ctx, mod_c, g_pre, g_post, w_up, cw, cb, w_dn, tm=ctx_len)
    return x
```

```python
import functools
import math

import numpy as np
import jax
import jax.numpy as jnp
from jax import lax
from jax.experimental import pallas as pl
from jax.experimental.pallas import tpu as pltpu

D_MODEL = 1024
GRID_W = 64
N_MOD = 6
RG_HEADS = 8
RG_BLOCK = D_MODEL // RG_HEADS
RG_CONV = 4
RG_C = 8.0
POOL_WINDOWS = (2, 4, 8, 16)
POOL_GROUP = D_MODEL // len(POOL_WINDOWS)
D_FF = 3 * D_MODEL
EPS = 1e-6
TINY = 1e-30

SUBLANES = 8
LANES = 128
BF16_ROWS = 16
MXU_N = 512
HEADS_PER_TILE = MXU_N // RG_BLOCK
HALO = 8
POOL_CHUNK = 256
OUT_PARTS = 4
SCAN_SEGS = 16
VMEM_LIMIT = 60 * 1024 * 1024

F32 = jnp.float32
BF16 = jnp.bfloat16


def _norm_mod(v, gm, shift):
    r = lax.rsqrt(jnp.mean(v * v, axis=-1, keepdims=True) + EPS)
    y = (v * r) * gm
    return y if shift is None else y + shift


def _gelu(x):
    k1 = math.sqrt(2.0 / math.pi)
    hx = 0.5 * x
    return hx + hx * jnp.tanh(x * (k1 + (k1 * 0.044715) * (x * x)))


def _sigmoid(x):
    return 1.0 / (1.0 + jnp.exp(-x))


def _const_spec(shape):
    nd = len(shape)
    return pl.BlockSpec(shape, lambda *_: (0,) * nd, pipeline_mode=pl.Buffered(1))


def _tile_specs(tm, nt, reverse):
    per = tm // HALO
    last = nt * per - 1

    def tile(j):
        return (nt - 1 - j) if reverse else j

    main = pl.BlockSpec((1, tm, D_MODEL), lambda b, j: (b, tile(j), 0))
    prev = pl.BlockSpec((1, HALO, D_MODEL), lambda b, j: (b, jnp.maximum(tile(j) * per - 1, 0), 0))
    nxt = pl.BlockSpec((1, HALO, D_MODEL), lambda b, j: (b, jnp.minimum((tile(j) + 1) * per, last), 0))
    return main, prev, nxt


def _mod_spec(n_mod_rows):
    if n_mod_rows == 1:
        return pl.BlockSpec((1, N_MOD, D_MODEL), lambda b, j: (0, 0, 0))
    return pl.BlockSpec((1, N_MOD, D_MODEL), lambda b, j: (b, 0, 0))


def _ada_kernel(ct_ref, w_ref, b_ref, o_ref, *, n_rows):
    ct = ct_ref[...]
    st = ct * _sigmoid(ct)
    w = w_ref[0]
    rows = [jnp.sum(w * st[:, m:m + 1], axis=0, keepdims=True) for m in range(n_rows)]
    rows += [jnp.zeros_like(rows[0])] * (SUBLANES - n_rows)
    o_ref[0] = jnp.concatenate(rows, axis=0) + b_ref[0]


def _ada(ct, ada_w, ada_b, n_rows):
    depth, d, n = ada_w.shape
    tn = 1536
    return pl.pallas_call(
        functools.partial(_ada_kernel, n_rows=n_rows),
        grid=(depth, n // tn),
        in_specs=[pl.BlockSpec((d, SUBLANES), lambda i, k: (0, 0)),
                  pl.BlockSpec((1, d, tn), lambda i, k: (i, 0, k)),
                  pl.BlockSpec((1, 1, tn), lambda i, k: (i, 0, k))],
        out_specs=pl.BlockSpec((1, SUBLANES, tn), lambda i, k: (i, 0, k)),
        out_shape=jax.ShapeDtypeStruct((depth, SUBLANES, n), F32),
        compiler_params=pltpu.CompilerParams(dimension_semantics=("parallel", "parallel"),
                                             vmem_limit_bytes=VMEM_LIMIT),
        name="ada_mod",
    )(ct, ada_w, ada_b.reshape(depth, 1, n))


def _prep_ext(x_ref, xp_ref, xn_ref, g_pre, shift, scale, tile, nt):
    gm = g_pre * (1.0 + scale)

    def prep(v):
        return _norm_mod(v, gm, shift)

    hp = jnp.where(tile > 0, prep(xp_ref[0]), 0.0)
    hn = jnp.where(tile < nt - 1, prep(xn_ref[0]), 0.0)
    return jnp.concatenate([hp, prep(x_ref[0]), hn], axis=0)


def _shifted(slab_ref, slab, k, tm):
    if k == 0:
        return slab_ref[slab, pl.ds(HALO, tm), :]
    return slab_ref[slab, pl.ds(HALO + k, tm, stride=1), :]


def _ffn_kernel(x_ref, xp_ref, xn_ref, mod_ref, gpre_ref, gpost_ref, wup_ref, cw_ref, cb_ref, wdn_ref,
                o_ref, hext_ref, zs0_ref, zs1_ref, zv0_ref, zv1_ref, act_ref, *, tm, nt, fc):
    j = pl.program_id(1)
    shift, scale, gate = mod_ref[0, 3:4, :], mod_ref[0, 4:5, :], mod_ref[0, 5:6, :]
    hext = _prep_ext(x_ref, xp_ref, xn_ref, gpre_ref[...], shift, scale, j, nt)
    hext_ref[...] = hext.astype(BF16)
    ns = fc // LANES
    n_chunks = D_FF // fc
    zs_refs, zv_refs = (zs0_ref, zs1_ref), (zv0_ref, zv1_ref)

    def up(c):
        zs_ref = zs_refs[c % 2]
        z = jnp.dot(hext_ref[...], wup_ref[:, 2 * c * fc:2 * (c + 1) * fc], preferred_element_type=F32)
        for l in range(ns):
            zs_ref[l] = z[:, l * LANES:(l + 1) * LANES]
        zv_refs[c % 2][...] = z[HALO:HALO + tm, fc:]

    def act(c):
        lo, zs_ref, zv_ref = c * fc, zs_refs[c % 2], zv_refs[c % 2]
        for l in range(ns):
            ln = slice(lo + l * LANES, lo + (l + 1) * LANES)
            cw = cw_ref[:, ln]
            u = (cw[0:1] * _shifted(zs_ref, l, -1, tm) + cw[1:2] * _shifted(zs_ref, l, 0, tm)
                 + cw[2:3] * _shifted(zs_ref, l, 1, tm) + cb_ref[:, ln])
            act_ref[:, ln] = (_gelu(u) * zv_ref[:, l * LANES:(l + 1) * LANES]).astype(BF16)

    up(0)
    for c in range(n_chunks):
        if c + 1 < n_chunks:
            up(c + 1)
        act(c)
    out = jnp.dot(act_ref[...], wdn_ref[...], preferred_element_type=F32)
    o_ref[0] = x_ref[0] + _norm_mod(out, gate * gpost_ref[...], None)


FFN_CHUNK = 1024


def _pack_up_weights(w_up, fc):
    d = w_up.shape[0]
    w = w_up.astype(BF16).reshape(d, 2, D_FF // fc, fc)
    return w.transpose(0, 2, 1, 3).reshape(d, 2 * D_FF)


def _ffn(x, mod, g_pre, g_post, w_up, conv_w, conv_b, w_down, *, tm, fc=FFN_CHUNK):
    b, t, d = x.shape
    nt = t // tm
    main, prev, nxt = _tile_specs(tm, nt, False)
    kern = functools.partial(_ffn_kernel, tm=tm, nt=nt, fc=fc)
    assert w_up.shape == (d, 2 * D_FF)
    return pl.pallas_call(
        kern,
        grid=(b, nt),
        in_specs=[main, prev, nxt, _mod_spec(mod.shape[0]),
                  _const_spec((1, d)), _const_spec((1, d)),
                  _const_spec(w_up.shape), _const_spec(conv_w.shape), _const_spec((1, D_FF)),
                  _const_spec(w_down.shape)],
        out_specs=pl.BlockSpec((1, tm, d), lambda bb, j: (bb, j, 0)),
        out_shape=jax.ShapeDtypeStruct(x.shape, F32),
        scratch_shapes=[pltpu.VMEM((tm + 2 * HALO, d), BF16),
                        pltpu.VMEM((fc // LANES, tm + 2 * HALO, LANES), F32),
                        pltpu.VMEM((fc // LANES, tm + 2 * HALO, LANES), F32),
                        pltpu.VMEM((tm, fc), F32), pltpu.VMEM((tm, fc), F32), pltpu.VMEM((tm, D_FF), BF16)],
        compiler_params=pltpu.CompilerParams(dimension_semantics=("parallel", "parallel"),
                                             vmem_limit_bytes=VMEM_LIMIT),
        name="conv_ffn",
    )(x, x, x, mod, g_pre, g_post, w_up, conv_w, conv_b, w_down)


def _seg_pitch(seg):
    p = seg + 4
    assert p % 8 == 4, seg
    return p


def _scan_tile(a_scr, b_scr, h_scr, carry_ref, *, seg, reverse):
    pitch = _seg_pitch(seg)
    nh = a_scr.shape[0]
    nv = SCAN_SEGS // SUBLANES
    sub = lax.broadcasted_iota(jnp.int32, (SUBLANES, RG_BLOCK), 0)
    chains = [(hd, v) for hd in range(nh) for v in range(nv)]

    def rows(v, i):
        return pl.ds(v * SUBLANES * pitch + ((seg - 1 - i) if reverse else i), SUBLANES, stride=pitch)

    def pass1(i, st):
        out = []
        for k, (hd, v) in enumerate(chains):
            av = a_scr[hd, rows(v, i), :]
            out += [av * st[2 * k] + b_scr[hd, rows(v, i), :], st[2 * k + 1] * av]
        return tuple(out)

    init = tuple(jnp.zeros((SUBLANES, RG_BLOCK), F32) if k % 2 == 0 else jnp.ones((SUBLANES, RG_BLOCK), F32)
                 for k in range(2 * len(chains)))
    st = init
    for i in range(seg):
        st = pass1(i, st)

    first = SUBLANES - 1 if reverse else 0
    last = 0 if reverse else SUBLANES - 1
    cins = {}
    for hd in range(nh):
        carry = jnp.broadcast_to(carry_ref[hd], (SUBLANES, RG_BLOCK))
        for v in (reversed(range(nv)) if reverse else range(nv)):
            k = hd * nv + v
            e, pe = st[2 * k], st[2 * k + 1]
            cin = carry
            for _ in range(SUBLANES - 1):
                nxt = pltpu.roll(e + pe * cin, (SUBLANES - 1) if reverse else 1, axis=0)
                cin = jnp.where(sub == first, carry, nxt)
            cins[(hd, v)] = cin
            end = e + pe * cin
            carry = jnp.broadcast_to(end[last:last + 1, :], (SUBLANES, RG_BLOCK))
        carry_ref[hd] = carry[0:1, :]

    def pass2(i, hs):
        out = []
        for k, (hd, v) in enumerate(chains):
            h = a_scr[hd, rows(v, i), :] * hs[k] + b_scr[hd, rows(v, i), :]
            h_scr[hd, rows(v, i), :] = h
            out.append(h)
        return tuple(out)

    hs = tuple(cins[ch] for ch in chains)
    for i in range(seg):
        hs = pass2(i, hs)


def _rg_gates_scan(uh_of_head, after_gate_matmul, wg_ref, ba_ref, bx_ref, lam_ref, h0_ref, carry_out_ref,
                   a_scr, b_scr, h_scr, carry_ref, *, tm, nt, reverse):
    j = pl.program_id(1)
    seg = tm // SCAN_SEGS
    pitch = _seg_pitch(seg)

    @pl.when(j == 0)
    def _():
        for hd in range(RG_HEADS):
            carry_ref[hd] = h0_ref[0, 0:1, hd * RG_BLOCK:(hd + 1) * RG_BLOCK]

    lam = lam_ref[...]
    sp = jnp.maximum(-lam, 0.0) + jnp.log(1.0 + jnp.exp(-jnp.abs(lam)))
    nc = (-0.5 * RG_C * math.log2(math.e)) * sp
    half_ba, half_bx = 0.5 * ba_ref[...], 0.5 * bx_ref[...]
    def gate_matmul(pair):
        uhs = [uh_of_head(2 * pair + q) for q in range(2)]
        gz = jnp.dot(jnp.concatenate(uhs, axis=1).astype(BF16), wg_ref[pair], preferred_element_type=F32)
        after_gate_matmul(2 * pair + 1)
        return uhs, gz

    n_pairs = RG_HEADS // 2
    queued = gate_matmul(0)
    for pair in range(n_pairs):
        uhs, gz = queued
        if pair + 1 < n_pairs:
            queued = gate_matmul(pair + 1)
        for q in range(2):
            hd, uh = 2 * pair + q, uhs[q]
            ln = slice(hd * RG_BLOCK, (hd + 1) * RG_BLOCK)
            tr = jnp.tanh(gz[:, 2 * q * RG_BLOCK:(2 * q + 1) * RG_BLOCK] + half_ba[:, ln])
            ti = jnp.tanh(gz[:, (2 * q + 1) * RG_BLOCK:(2 * q + 2) * RG_BLOCK] + half_bx[:, ln])
            a = jnp.exp2(nc[:, ln] + nc[:, ln] * tr)
            y1 = 1.0 - a * a
            mult = y1 * lax.rsqrt(jnp.maximum(y1, TINY))
            bb = mult * (uh + uh * ti)
            for s in range(SCAN_SEGS):
                a_scr[hd, pl.ds(s * pitch, seg, stride=1), :] = a[s * seg:(s + 1) * seg]
                b_scr[hd, pl.ds(s * pitch, seg, stride=1), :] = bb[s * seg:(s + 1) * seg]

    _scan_tile(a_scr, b_scr, h_scr, carry_ref, seg=seg, reverse=reverse)

    @pl.when(j == nt - 1)
    def _():
        for hd in range(RG_HEADS):
            carry_out_ref[0, :, hd * RG_BLOCK:(hd + 1) * RG_BLOCK] = jnp.broadcast_to(
                carry_ref[hd], (SUBLANES, RG_BLOCK))


def _head_h(h_scr, hd, tm):
    seg = tm // SCAN_SEGS
    pitch = _seg_pitch(seg)
    return jnp.concatenate([h_scr[hd, pl.ds(s * pitch, seg, stride=1), :] for s in range(SCAN_SEGS)], axis=0)


def _rg_scan_kernel(x_ref, xp_ref, xn_ref, mod_ref, gpre_ref, wr_ref, cw_ref, cb_ref, wg_ref, ba_ref, bx_ref,
                    lam_ref, h0_ref, h_ref, uh_ref, carry_out_ref, hext_scr, rec0_scr, rec1_scr, a_scr, b_scr,
                    h_scr, carry_ref, *, tm, nt):
    tile = nt - 1 - pl.program_id(1)
    shift, scale = mod_ref[0, 0:1, :], mod_ref[0, 1:2, :]
    gm = gpre_ref[...] * (1.0 + scale)
    half_cw, half_cb = 0.5 * cw_ref[...], 0.5 * cb_ref[...]

    rec_scrs = (rec0_scr, rec1_scr)
    assert len(rec_scrs) * HEADS_PER_TILE == RG_HEADS
    n_ext = tm + 2 * HALO
    rows_a = -(-(HALO + tm // 2) // BF16_ROWS) * BF16_ROWS

    def rec_chunk(p, lo, hi):
        r = jnp.dot(hext_scr[lo:hi], wr_ref[:, p * MXU_N:(p + 1) * MXU_N], preferred_element_type=F32)
        for q in range(HEADS_PER_TILE):
            rec_scrs[p][q, lo:hi] = r[:, q * RG_BLOCK:(q + 1) * RG_BLOCK]

    hp = jnp.where(tile > 0, _norm_mod(xp_ref[0], gm, shift), 0.0)
    hext_scr[:rows_a] = jnp.concatenate([hp, _norm_mod(x_ref[0, :rows_a - HALO], gm, shift)],
                                        axis=0).astype(BF16)
    rec_chunk(0, 0, rows_a)
    hn = jnp.where(tile < nt - 1, _norm_mod(xn_ref[0], gm, shift), 0.0)
    hext_scr[rows_a:] = jnp.concatenate([_norm_mod(x_ref[0, rows_a - HALO:], gm, shift), hn],
                                        axis=0).astype(BF16)
    rec_chunk(0, rows_a, n_ext)

    def after_gate_matmul(hd):
        p, q = divmod(hd, HEADS_PER_TILE)
        if q == HEADS_PER_TILE - 1 and p + 1 < len(rec_scrs):
            rec_chunk(p + 1, 0, n_ext)

    def uh_of_head(hd):
        p, q = divmod(hd, HEADS_PER_TILE)
        ln = slice(hd * RG_BLOCK, (hd + 1) * RG_BLOCK)
        uh = half_cb[:, ln] + sum(half_cw[k:k + 1, ln] * _shifted(rec_scrs[p], q, k - RG_CONV // 2, tm)
                                  for k in range(RG_CONV))
        uh_ref[0, :, ln] = uh
        return uh

    _rg_gates_scan(uh_of_head, after_gate_matmul, wg_ref, ba_ref, bx_ref, lam_ref, h0_ref, carry_out_ref,
                   a_scr, b_scr, h_scr, carry_ref, tm=tm, nt=nt, reverse=True)
    for hd in range(RG_HEADS):
        h_ref[0, :, hd * RG_BLOCK:(hd + 1) * RG_BLOCK] = _head_h(h_scr, hd, tm)


def _rg_final_kernel(x_ref, mod_ref, gpre_ref, uh_ref, wg_ref, ba_ref, bx_ref, lam_ref, h0_ref, hb_ref,
                     wgate_ref, wout_ref, gpost_ref, o_ref, carry_out_ref, a_scr, b_scr, h_scr, carry_ref,
                     hm_scr, gz_scr, gy_scr, *, tm, nt):
    shift, scale, gate = mod_ref[0, 0:1, :], mod_ref[0, 1:2, :], mod_ref[0, 2:3, :]
    hm_scr[...] = _norm_mod(x_ref[0], gpre_ref[...] * (1.0 + scale), shift).astype(BF16)

    def after_gate_matmul(hd):
        pass

    def uh_of_head(hd):
        if hd % 2 == 0:
            ln2 = slice(hd * RG_BLOCK, (hd + 2) * RG_BLOCK)
            gz_scr[:, ln2] = jnp.dot(hm_scr[...], wgate_ref[:, ln2], preferred_element_type=F32)
        return uh_ref[0, :, hd * RG_BLOCK:(hd + 1) * RG_BLOCK]

    _rg_gates_scan(uh_of_head, after_gate_matmul, wg_ref, ba_ref, bx_ref, lam_ref, h0_ref, carry_out_ref,
                   a_scr, b_scr, h_scr, carry_ref, tm=tm, nt=nt, reverse=False)
    out = None
    for part in range(OUT_PARTS):
        kk = slice(part * (D_MODEL // OUT_PARTS), (part + 1) * (D_MODEL // OUT_PARTS))
        for hd in range(part * RG_HEADS // OUT_PARTS, (part + 1) * RG_HEADS // OUT_PARTS):
            ln = slice(hd * RG_BLOCK, (hd + 1) * RG_BLOCK)
            gy_scr[:, ln] = (_gelu(gz_scr[:, ln]) * (_head_h(h_scr, hd, tm) + hb_ref[0, :, ln])).astype(BF16)
        partial = jnp.dot(gy_scr[:, kk], wout_ref[kk, :], preferred_element_type=F32)
        out = partial if out is None else out + partial
    o_ref[0] = x_ref[0] + _norm_mod(out, gate * gpost_ref[...], None)


def _scan_scratch(tm):
    rows = SCAN_SEGS * _seg_pitch(tm // SCAN_SEGS)
    return [pltpu.VMEM((RG_HEADS, rows, RG_BLOCK), F32)] * 3 + [pltpu.VMEM((RG_HEADS, 1, RG_BLOCK), F32)]


def _rg_reverse(x, mod, g_pre, w_rec, conv_w, conv_b, w_gates, b_a, b_x, lam, h0, *, tm):
    b, t, d = x.shape
    nt = t // tm
    main, prev, nxt = _tile_specs(tm, nt, True)
    carry_spec = pl.BlockSpec((1, SUBLANES, d), lambda bb, j: (bb, 0, 0))
    return pl.pallas_call(
        functools.partial(_rg_scan_kernel, tm=tm, nt=nt),
        grid=(b, nt),
        in_specs=[main, prev, nxt, _mod_spec(mod.shape[0]), _const_spec((1, d)), _const_spec(w_rec.shape),
                  _const_spec(conv_w.shape), _const_spec((1, d)), _const_spec(w_gates.shape),
                  _const_spec((1, d)), _const_spec((1, d)), _const_spec((1, d)), carry_spec],
        out_specs=[main, main, carry_spec],
        out_shape=[jax.ShapeDtypeStruct(x.shape, F32), jax.ShapeDtypeStruct(x.shape, F32),
                   jax.ShapeDtypeStruct((b, SUBLANES, d), F32)],
        scratch_shapes=[pltpu.VMEM((tm + 2 * HALO, d), BF16),
                        pltpu.VMEM((HEADS_PER_TILE, tm + 2 * HALO, RG_BLOCK), F32),
                        pltpu.VMEM((HEADS_PER_TILE, tm + 2 * HALO, RG_BLOCK), F32)] + _scan_scratch(tm),
        compiler_params=pltpu.CompilerParams(dimension_semantics=("arbitrary", "arbitrary"),
                                             vmem_limit_bytes=VMEM_LIMIT),
        name="rg_scan",
    )(x, x, x, mod, g_pre, w_rec, conv_w, conv_b, w_gates, b_a, b_x, lam, h0)


def _rg_forward(x, mod, g_pre, uh, w_gates, b_a, b_x, lam, h0, h_rev, w_gate, w_out, g_post, *, tm):
    b, t, d = x.shape
    nt = t // tm
    main = pl.BlockSpec((1, tm, d), lambda bb, j: (bb, j, 0))
    carry_spec = pl.BlockSpec((1, SUBLANES, d), lambda bb, j: (bb, 0, 0))
    return pl.pallas_call(
        functools.partial(_rg_final_kernel, tm=tm, nt=nt),
        grid=(b, nt),
        in_specs=[main, _mod_spec(mod.shape[0]), _const_spec((1, d)), main, _const_spec(w_gates.shape),
                  _const_spec((1, d)), _const_spec((1, d)), _const_spec((1, d)), carry_spec, main,
                  _const_spec(w_gate.shape), _const_spec(w_out.shape), _const_spec((1, d))],
        out_specs=[main, carry_spec],
        out_shape=[jax.ShapeDtypeStruct(x.shape, F32), jax.ShapeDtypeStruct((b, SUBLANES, d), F32)],
        scratch_shapes=_scan_scratch(tm) + [pltpu.VMEM((tm, d), BF16), pltpu.VMEM((tm, d), F32),
                                            pltpu.VMEM((tm, d), BF16)],
        compiler_params=pltpu.CompilerParams(dimension_semantics=("arbitrary", "arbitrary"),
                                             vmem_limit_bytes=VMEM_LIMIT),
        name="rg_final",
    )(x, mod, g_pre, uh, w_gates, b_a, b_x, lam, h0, h_rev, w_gate, w_out, g_post)


def _pool_kernel(*refs, tm, nt, width, halo_rows):
    if halo_rows:
        (x_ref, xp_ref, xn_ref, mod_ref, gpre_ref, gpost_ref, band_ref, invcc_ref, rowinv_ref, pw_ref, pb_ref,
         ps_ref, o_ref, hext_scr, rs_scr) = refs
    else:
        (x_ref, mod_ref, gpre_ref, gpost_ref, band_ref, invcc_ref, rowinv_ref, pw_ref, pb_ref, ps_ref,
         o_ref, hext_scr, rs_scr) = refs
    j = pl.program_id(1)
    rt = tm // width
    shift, scale, gate = mod_ref[0, 0:1, :], mod_ref[0, 1:2, :], mod_ref[0, 2:3, :]
    gm = gpre_ref[...] * (1.0 + scale)

    def prep(v):
        return _norm_mod(v, gm, shift)

    base = halo_rows * width
    hext_scr[base:base + tm] = prep(x_ref[0])
    def halo_row(src_ref, i, dst_row, reach, valid):
        groups = [g for g, w in enumerate(POOL_WINDOWS) if w // 2 >= reach]
        if groups:
            lanes = slice(groups[0] * POOL_GROUP, D_MODEL)
            v = src_ref[0, i * width:(i + 1) * width, :]
            r = lax.rsqrt(jnp.mean(v * v, axis=-1, keepdims=True) + EPS)
            y = (v[:, lanes] * r) * gm[:, lanes] + shift[:, lanes]
            hext_scr[dst_row * width:(dst_row + 1) * width, lanes] = jnp.where(valid, y, 0.0)

    for i in range(halo_rows):
        halo_row(xp_ref, i, i, halo_rows - i, j > 0)
        halo_row(xn_ref, i, halo_rows + rt + i, i + 2, j < nt - 1)

    def grid_row(r, ln):
        return hext_scr[(r + halo_rows) * width:(r + halo_rows + 1) * width, ln]

    for g, w in enumerate(POOL_WINDOWS):
        ln = slice(g * POOL_GROUP, (g + 1) * POOL_GROUP)
        if halo_rows:
            win = grid_row(-(w // 2), ln)
            for dr in range(-(w // 2) + 1, w // 2):
                win = win + grid_row(dr, ln)
        else:
            win = grid_row(0, ln)
        for r in range(rt):
            rs_scr[r * width:(r + 1) * width, ln] = win * rowinv_ref[pl.ds(j * rt + r, 1), ln]
            if r + 1 < rt:
                win = win + grid_row(r + w // 2, ln) - grid_row(r - w // 2, ln)

    blocks = [(ch, g) for ch in range(tm // POOL_CHUNK) for g in range(len(POOL_WINDOWS))]

    def window_sums(ch, g):
        ln = slice(g * POOL_GROUP, (g + 1) * POOL_GROUP)
        rs = rs_scr[ch * POOL_CHUNK:(ch + 1) * POOL_CHUNK, ln]
        hi = rs.astype(BF16)
        lo = (rs - hi.astype(F32)).astype(BF16)
        return (jnp.dot(band_ref[g], hi, preferred_element_type=F32)
                + jnp.dot(band_ref[g], lo, preferred_element_type=F32))

    def group_map(ch, g, sums):
        ln = slice(g * POOL_GROUP, (g + 1) * POOL_GROUP)
        hg = hext_scr[base + ch * POOL_CHUNK:base + (ch + 1) * POOL_CHUNK, ln]
        yg = jnp.dot((sums * invcc_ref[:, ln] - hg).astype(BF16), pw_ref[g], preferred_element_type=F32)
        rs_scr[ch * POOL_CHUNK:(ch + 1) * POOL_CHUNK, ln] = (yg + pb_ref[:, ln]) * ps_ref[:, ln]

    sums = window_sums(*blocks[0])
    for k, blk in enumerate(blocks):
        nxt = window_sums(*blocks[k + 1]) if k + 1 < len(blocks) else None
        group_map(*blk, sums)
        sums = nxt

    o_ref[0] = x_ref[0] + _norm_mod(rs_scr[...], gate * gpost_ref[...], None)


def _pool_consts(n_rows, width):
    idx = np.arange(POOL_CHUNK)
    row, col = idx // width, idx % width
    band = np.zeros((len(POOL_WINDOWS), POOL_CHUNK, POOL_CHUNK), np.float32)
    invcc = np.zeros((POOL_CHUNK, D_MODEL), np.float32)
    rowinv = np.zeros((n_rows, D_MODEL), np.float32)
    r = np.arange(n_rows)
    for g, w in enumerate(POOL_WINDOWS):
        same = row[:, None] == row[None, :]
        d = col[None, :] - col[:, None]
        band[g] = (same & (d >= -(w // 2)) & (d < w // 2)).astype(np.float32)
        cc = np.minimum(col + w // 2, width) - np.maximum(col - w // 2, 0)
        invcc[:, g * POOL_GROUP:(g + 1) * POOL_GROUP] = (1.0 / cc)[:, None]
        cr = np.minimum(r + w // 2, n_rows) - np.maximum(r - w // 2, 0)
        rowinv[:, g * POOL_GROUP:(g + 1) * POOL_GROUP] = (1.0 / cr)[:, None]
    return jnp.asarray(band, BF16), jnp.asarray(invcc), jnp.asarray(rowinv)


def _pool(x, mod, g_pre, g_post, pw, pb, ps, *, width, tm):
    b, t, d = x.shape
    n_rows = t // width
    nt = t // tm
    halo_rows = max(POOL_WINDOWS) // 2 if nt > 1 else 0
    assert nt > 1 or n_rows == 1
    band, invcc, rowinv = _pool_consts(n_rows, width)
    main = pl.BlockSpec((1, tm, d), lambda bb, j: (bb, j, 0))
    in_specs, args = [main], [x]
    if halo_rows:
        hb = halo_rows * width
        per = tm // hb
        last = t // hb - 1
        in_specs += [pl.BlockSpec((1, hb, d), lambda bb, j: (bb, jnp.maximum(j * per - 1, 0), 0)),
                     pl.BlockSpec((1, hb, d), lambda bb, j: (bb, jnp.minimum((j + 1) * per, last), 0))]
        args += [x, x]
    in_specs += [_mod_spec(mod.shape[0]), _const_spec((1, d)), _const_spec((1, d)), _const_spec(band.shape),
                 _const_spec(invcc.shape), _const_spec(rowinv.shape), _const_spec(pw.shape),
                 _const_spec((1, d)), _const_spec((1, d))]
    args += [mod, g_pre, g_post, band, invcc, rowinv, pw, pb, ps]
    kern = functools.partial(_pool_kernel, tm=tm, nt=nt, width=width, halo_rows=halo_rows)
    return pl.pallas_call(
        kern,
        grid=(b, nt),
        in_specs=in_specs,
        out_specs=main,
        out_shape=jax.ShapeDtypeStruct(x.shape, F32),
        scratch_shapes=[pltpu.VMEM((tm + 2 * halo_rows * width, d), F32), pltpu.VMEM((tm, d), F32)],
        compiler_params=pltpu.CompilerParams(dimension_semantics=("parallel", "parallel"),
                                             vmem_limit_bytes=VMEM_LIMIT),
        name="pool_mix",
    )(*args)


def kernel(x, c, ctx, c_ctx, ada_w, ada_b, norm_pre_mix, norm_post_mix, norm_pre_ffn, norm_post_ffn, rg_w_in,
           rg_conv_w, rg_conv_b, rg_gate_a_w, rg_gate_a_b, rg_gate_x_w, rg_gate_x_b, rg_lambda, rg_w_out, pool_w,
           pool_b, pool_scale, ffn_w_up, ffn_conv_w, ffn_conv_b, ffn_w_down):
    bsz, seq, d = x.shape
    ctx_len = ctx.shape[1]
    depth = ada_w.shape[0]
    assert d == D_MODEL and seq % GRID_W == 0

    ct = jnp.zeros((d, SUBLANES), F32).at[:, :bsz].set(c.T).at[:, bsz].set(c_ctx)
    mods = _ada(ct, ada_w, ada_b, bsz + 1)
    row = lambda a, i: a[i].reshape(1, -1)
    tm_lat = 512
    zeros_state = jnp.zeros((bsz, SUBLANES, d), F32)

    for i in range(depth):
        last = i == depth - 1
        mod_l = mods[i, :bsz].reshape(bsz, N_MOD, d)
        mod_c = mods[i, bsz].reshape(1, N_MOD, d)
        g_pre, g_post = row(norm_pre_mix, i), row(norm_post_mix, i)
        jm = i // 2
        if i % 2 == 0:
            w_gate = rg_w_in[jm, :, :d].astype(BF16)
            w_rec = rg_w_in[jm, :, d:].astype(BF16)
            w_out = rg_w_out[jm].astype(BF16)
            conv_w, conv_b = rg_conv_w[jm], row(rg_conv_b, jm)
            dirs = []
            for dd in range(2):
                wg = jnp.concatenate([rg_gate_a_w[jm, dd], rg_gate_x_w[jm, dd]], axis=-1)
                zg = jnp.zeros_like(wg[0::2])
                wg = jnp.concatenate([jnp.concatenate([wg[0::2], zg], axis=-1),
                                      jnp.concatenate([zg, wg[1::2]], axis=-1)], axis=1).astype(BF16)
                dirs.append((wg, rg_gate_a_b[jm, dd].reshape(1, d), rg_gate_x_b[jm, dd].reshape(1, d),
                             rg_lambda[jm, dd].reshape(1, d)))
            hb_c, uh_c, end_b = _rg_reverse(ctx, mod_c, g_pre, w_rec, conv_w, conv_b, *dirs[1], zeros_state,
                                            tm=ctx_len)
            ctx_new, end_f = _rg_forward(ctx, mod_c, g_pre, uh_c, *dirs[0], zeros_state, hb_c, w_gate, w_out,
                                         g_post, tm=ctx_len)
            hb_l, uh_l, _ = _rg_reverse(x, mod_l, g_pre, w_rec, conv_w, conv_b, *dirs[1], end_b, tm=1024)
            x, _ = _rg_forward(x, mod_l, g_pre, uh_l, *dirs[0], end_f, hb_l, w_gate, w_out, g_post, tm=tm_lat)
            if not last:
                ctx = ctx_new
        else:
            pw = pool_w[jm].astype(BF16)
            pb, ps = row(pool_b, jm), row(pool_scale, jm)
            x_new = _pool(x, mod_l, g_pre, g_post, pw, pb, ps, width=GRID_W, tm=1024)
            if not last:
                ctx = _pool(ctx, mod_c, g_pre, g_post, pw, pb, ps, width=ctx_len, tm=ctx_len)
            x = x_new

        g_pre, g_post = row(norm_pre_ffn, i), row(norm_post_ffn, i)
        w_up, w_dn = _pack_up_weights(ffn_w_up[i], FFN_CHUNK), ffn_w_down[i].astype(BF16)
        cw, cb = ffn_conv_w[i], row(ffn_conv_b, i)
        x = _ffn(x, mod_l, g_pre, g_post, w_up, cw, cb, w_dn, tm=1024)
        if not last:
            ctx = _ffn(ctx, mod_c, g_pre, g_post, w_up, cw, cb, w_dn, tm=ctx_len)
    return x
```

```python
import functools
import math

import numpy as np
import jax
import jax.numpy as jnp
from jax import lax
from jax.experimental import pallas as pl
from jax.experimental.pallas import tpu as pltpu

D_MODEL = 1024
GRID_W = 64
N_MOD = 6
RG_HEADS = 8
RG_BLOCK = D_MODEL // RG_HEADS
RG_CONV = 4
RG_C = 8.0
POOL_WINDOWS = (2, 4, 8, 16)
POOL_GROUP = D_MODEL // len(POOL_WINDOWS)
D_FF = 3 * D_MODEL
EPS = 1e-6
TINY = 1e-30

SUBLANES = 8
LANES = 128
BF16_ROWS = 16
MXU_N = 512
HEADS_PER_TILE = MXU_N // RG_BLOCK
HALO = 8
POOL_CHUNK = 256
OUT_PARTS = 4
SCAN_SEGS = 16
VMEM_LIMIT = 60 * 1024 * 1024

F32 = jnp.float32
BF16 = jnp.bfloat16


def _norm_mod(v, gm, shift):
    r = lax.rsqrt(jnp.mean(v * v, axis=-1, keepdims=True) + EPS)
    y = (v * r) * gm
    return y if shift is None else y + shift


def _gelu(x):
    k1 = math.sqrt(2.0 / math.pi)
    hx = 0.5 * x
    return hx + hx * jnp.tanh(x * (k1 + (k1 * 0.044715) * (x * x)))


def _sigmoid(x):
    return 1.0 / (1.0 + jnp.exp(-x))


def _const_spec(shape):
    nd = len(shape)
    return pl.BlockSpec(shape, lambda *_: (0,) * nd, pipeline_mode=pl.Buffered(1))


def _tile_specs(tm, nt, reverse):
    per = tm // HALO
    last = nt * per - 1

    def tile(j):
        return (nt - 1 - j) if reverse else j

    main = pl.BlockSpec((1, tm, D_MODEL), lambda b, j: (b, tile(j), 0))
    prev = pl.BlockSpec((1, HALO, D_MODEL), lambda b, j: (b, jnp.maximum(tile(j) * per - 1, 0), 0))
    nxt = pl.BlockSpec((1, HALO, D_MODEL), lambda b, j: (b, jnp.minimum((tile(j) + 1) * per, last), 0))
    return main, prev, nxt


def _mod_spec(n_mod_rows):
    if n_mod_rows == 1:
        return pl.BlockSpec((1, N_MOD, D_MODEL), lambda b, j: (0, 0, 0))
    return pl.BlockSpec((1, N_MOD, D_MODEL), lambda b, j: (b, 0, 0))


def _ada_kernel(ct_ref, w_ref, b_ref, o_ref, *, n_rows):
    ct = ct_ref[...]
    st = ct * _sigmoid(ct)
    w = w_ref[0]
    rows = [jnp.sum(w * st[:, m:m + 1], axis=0, keepdims=True) for m in range(n_rows)]
    rows += [jnp.zeros_like(rows[0])] * (SUBLANES - n_rows)
    o_ref[0] = jnp.concatenate(rows, axis=0) + b_ref[0]


def _ada(ct, ada_w, ada_b, n_rows):
    depth, d, n = ada_w.shape
    tn = 1536
    return pl.pallas_call(
        functools.partial(_ada_kernel, n_rows=n_rows),
        grid=(depth, n // tn),
        in_specs=[pl.BlockSpec((d, SUBLANES), lambda i, k: (0, 0)),
                  pl.BlockSpec((1, d, tn), lambda i, k: (i, 0, k)),
                  pl.BlockSpec((1, 1, tn), lambda i, k: (i, 0, k))],
        out_specs=pl.BlockSpec((1, SUBLANES, tn), lambda i, k: (i, 0, k)),
        out_shape=jax.ShapeDtypeStruct((depth, SUBLANES, n), F32),
        compiler_params=pltpu.CompilerParams(dimension_semantics=("parallel", "parallel"),
                                             vmem_limit_bytes=VMEM_LIMIT),
        name="ada_mod",
    )(ct, ada_w, ada_b.reshape(depth, 1, n))


def _prep_ext(x_ref, xp_ref, xn_ref, g_pre, shift, scale, tile, nt):
    gm = g_pre * (1.0 + scale)

    def prep(v):
        return _norm_mod(v, gm, shift)

    hp = jnp.where(tile > 0, prep(xp_ref[0]), 0.0)
    hn = jnp.where(tile < nt - 1, prep(xn_ref[0]), 0.0)
    return jnp.concatenate([hp, prep(x_ref[0]), hn], axis=0)


def _shifted(slab_ref, slab, k, tm):
    if k == 0:
        return slab_ref[slab, pl.ds(HALO, tm), :]
    return slab_ref[slab, pl.ds(HALO + k, tm, stride=1), :]


def _ffn_kernel(x_ref, xp_ref, xn_ref, mod_ref, gpre_ref, gpost_ref, wup_ref, cw_ref, cb_ref, wdn_ref,
                o_ref, hext_ref, zs0_ref, zs1_ref, zv0_ref, zv1_ref, act_ref, *, tm, nt, fc):
    j = pl.program_id(1)
    shift, scale, gate = mod_ref[0, 3:4, :], mod_ref[0, 4:5, :], mod_ref[0, 5:6, :]
    hext = _prep_ext(x_ref, xp_ref, xn_ref, gpre_ref[...], shift, scale, j, nt)
    hext_ref[...] = hext.astype(BF16)
    ns = fc // LANES
    n_chunks = D_FF // fc
    zs_refs, zv_refs = (zs0_ref, zs1_ref), (zv0_ref, zv1_ref)

    def up(c):
        zs_ref = zs_refs[c % 2]
        z = jnp.dot(hext_ref[...], wup_ref[:, 2 * c * fc:2 * (c + 1) * fc], preferred_element_type=F32)
        for l in range(ns):
            zs_ref[l] = z[:, l * LANES:(l + 1) * LANES]
        zv_refs[c % 2][...] = z[HALO:HALO + tm, fc:]

    def act(c):
        lo, zs_ref, zv_ref = c * fc, zs_refs[c % 2], zv_refs[c % 2]
        for l in range(ns):
            ln = slice(lo + l * LANES, lo + (l + 1) * LANES)
            cw = cw_ref[:, ln]
            u = (cw[0:1] * _shifted(zs_ref, l, -1, tm) + cw[1:2] * _shifted(zs_ref, l, 0, tm)
                 + cw[2:3] * _shifted(zs_ref, l, 1, tm) + cb_ref[:, ln])
            act_ref[:, ln] = (_gelu(u) * zv_ref[:, l * LANES:(l + 1) * LANES]).astype(BF16)

    up(0)
    for c in range(n_chunks):
        if c + 1 < n_chunks:
            up(c + 1)
        act(c)
    out = jnp.dot(act_ref[...], wdn_ref[...], preferred_element_type=F32)
    o_ref[0] = x_ref[0] + _norm_mod(out, gate * gpost_ref[...], None)


FFN_CHUNK = 1024


def _pack_kernel(w_ref, o_ref):
    o_ref[...] = w_ref[0].astype(BF16)


def _pack_up_weights(w_up_all, layer, fc):
    _, d, n2 = w_up_all.shape
    nc = D_FF // fc
    return pl.pallas_call(
        _pack_kernel,
        grid=(nc, 2),
        in_specs=[pl.BlockSpec((1, d, fc), lambda c, s: (layer, 0, s * nc + c))],
        out_specs=pl.BlockSpec((d, fc), lambda c, s: (0, 2 * c + s)),
        out_shape=jax.ShapeDtypeStruct((d, n2), BF16),
        compiler_params=pltpu.CompilerParams(dimension_semantics=("parallel", "parallel"),
                                             vmem_limit_bytes=VMEM_LIMIT),
        name="pack_up_weights",
    )(w_up_all)


def _ffn(x, mod, g_pre, g_post, w_up, conv_w, conv_b, w_down, *, tm, fc=FFN_CHUNK):
    b, t, d = x.shape
    nt = t // tm
    main, prev, nxt = _tile_specs(tm, nt, False)
    kern = functools.partial(_ffn_kernel, tm=tm, nt=nt, fc=fc)
    assert w_up.shape == (d, 2 * D_FF)
    return pl.pallas_call(
        kern,
        grid=(b, nt),
        in_specs=[main, prev, nxt, _mod_spec(mod.shape[0]),
                  _const_spec((1, d)), _const_spec((1, d)),
                  _const_spec(w_up.shape), _const_spec(conv_w.shape), _const_spec((1, D_FF)),
                  _const_spec(w_down.shape)],
        out_specs=pl.BlockSpec((1, tm, d), lambda bb, j: (bb, j, 0)),
        out_shape=jax.ShapeDtypeStruct(x.shape, F32),
        scratch_shapes=[pltpu.VMEM((tm + 2 * HALO, d), BF16),
                        pltpu.VMEM((fc // LANES, tm + 2 * HALO, LANES), F32),
                        pltpu.VMEM((fc // LANES, tm + 2 * HALO, LANES), F32),
                        pltpu.VMEM((tm, fc), F32), pltpu.VMEM((tm, fc), F32), pltpu.VMEM((tm, D_FF), BF16)],
        compiler_params=pltpu.CompilerParams(dimension_semantics=("parallel", "parallel"),
                                             vmem_limit_bytes=VMEM_LIMIT),
        name="conv_ffn",
    )(x, x, x, mod, g_pre, g_post, w_up, conv_w, conv_b, w_down)


def _seg_pitch(seg):
    p = seg + 4
    assert p % 8 == 4, seg
    return p


def _scan_tile(a_scr, b_scr, h_scr, carry_ref, *, seg, reverse):
    pitch = _seg_pitch(seg)
    nh = a_scr.shape[0]
    nv = SCAN_SEGS // SUBLANES
    sub = lax.broadcasted_iota(jnp.int32, (SUBLANES, RG_BLOCK), 0)
    chains = [(hd, v) for hd in range(nh) for v in range(nv)]

    def rows(v, i):
        return pl.ds(v * SUBLANES * pitch + ((seg - 1 - i) if reverse else i), SUBLANES, stride=pitch)

    def pass1(i, st):
        out = []
        for k, (hd, v) in enumerate(chains):
            av = a_scr[hd, rows(v, i), :]
            out += [av * st[2 * k] + b_scr[hd, rows(v, i), :], st[2 * k + 1] * av]
        return tuple(out)

    init = tuple(jnp.zeros((SUBLANES, RG_BLOCK), F32) if k % 2 == 0 else jnp.ones((SUBLANES, RG_BLOCK), F32)
                 for k in range(2 * len(chains)))
    st = init
    for i in range(seg):
        st = pass1(i, st)

    first = SUBLANES - 1 if reverse else 0
    last = 0 if reverse else SUBLANES - 1
    cins = {}
    for hd in range(nh):
        carry = jnp.broadcast_to(carry_ref[hd], (SUBLANES, RG_BLOCK))
        for v in (reversed(range(nv)) if reverse else range(nv)):
            k = hd * nv + v
            e, pe = st[2 * k], st[2 * k + 1]
            cin = carry
            for _ in range(SUBLANES - 1):
                nxt = pltpu.roll(e + pe * cin, (SUBLANES - 1) if reverse else 1, axis=0)
                cin = jnp.where(sub == first, carry, nxt)
            cins[(hd, v)] = cin
            end = e + pe * cin
            carry = jnp.broadcast_to(end[last:last + 1, :], (SUBLANES, RG_BLOCK))
        carry_ref[hd] = carry[0:1, :]

    def pass2(i, hs):
        out = []
        for k, (hd, v) in enumerate(chains):
            h = a_scr[hd, rows(v, i), :] * hs[k] + b_scr[hd, rows(v, i), :]
            h_scr[hd, rows(v, i), :] = h
            out.append(h)
        return tuple(out)

    hs = tuple(cins[ch] for ch in chains)
    for i in range(seg):
        hs = pass2(i, hs)


def _rg_gates_scan(uh_of_head, after_gate_matmul, wg_ref, ba_ref, bx_ref, lam_ref, h0_ref, carry_out_ref,
                   a_scr, b_scr, h_scr, carry_ref, *, tm, nt, reverse):
    j = pl.program_id(1)
    seg = tm // SCAN_SEGS
    pitch = _seg_pitch(seg)

    @pl.when(j == 0)
    def _():
        for hd in range(RG_HEADS):
            carry_ref[hd] = h0_ref[0, 0:1, hd * RG_BLOCK:(hd + 1) * RG_BLOCK]

    lam = lam_ref[...]
    sp = jnp.maximum(-lam, 0.0) + jnp.log(1.0 + jnp.exp(-jnp.abs(lam)))
    nc = (-0.5 * RG_C * math.log2(math.e)) * sp
    half_ba, half_bx = 0.5 * ba_ref[...], 0.5 * bx_ref[...]
    def gate_matmul(pair):
        uhs = [uh_of_head(2 * pair + q) for q in range(2)]
        gz = jnp.dot(jnp.concatenate(uhs, axis=1).astype(BF16), wg_ref[pair], preferred_element_type=F32)
        after_gate_matmul(2 * pair + 1)
        return uhs, gz

    n_pairs = RG_HEADS // 2
    queued = gate_matmul(0)
    for pair in range(n_pairs):
        uhs, gz = queued
        if pair + 1 < n_pairs:
            queued = gate_matmul(pair + 1)
        for q in range(2):
            hd, uh = 2 * pair + q, uhs[q]
            ln = slice(hd * RG_BLOCK, (hd + 1) * RG_BLOCK)
            tr = jnp.tanh(gz[:, 2 * q * RG_BLOCK:(2 * q + 1) * RG_BLOCK] + half_ba[:, ln])
            ti = jnp.tanh(gz[:, (2 * q + 1) * RG_BLOCK:(2 * q + 2) * RG_BLOCK] + half_bx[:, ln])
            a = jnp.exp2(nc[:, ln] + nc[:, ln] * tr)
            y1 = 1.0 - a * a
            mult = y1 * lax.rsqrt(jnp.maximum(y1, TINY))
            bb = mult * (uh + uh * ti)
            for s in range(SCAN_SEGS):
                a_scr[hd, pl.ds(s * pitch, seg, stride=1), :] = a[s * seg:(s + 1) * seg]
                b_scr[hd, pl.ds(s * pitch, seg, stride=1), :] = bb[s * seg:(s + 1) * seg]

    _scan_tile(a_scr, b_scr, h_scr, carry_ref, seg=seg, reverse=reverse)

    @pl.when(j == nt - 1)
    def _():
        for hd in range(RG_HEADS):
            carry_out_ref[0, :, hd * RG_BLOCK:(hd + 1) * RG_BLOCK] = jnp.broadcast_to(
                carry_ref[hd], (SUBLANES, RG_BLOCK))


def _head_h(h_scr, hd, tm):
    seg = tm // SCAN_SEGS
    pitch = _seg_pitch(seg)
    return jnp.concatenate([h_scr[hd, pl.ds(s * pitch, seg, stride=1), :] for s in range(SCAN_SEGS)], axis=0)


def _rg_scan_kernel(x_ref, xp_ref, xn_ref, mod_ref, gpre_ref, wr_ref, cw_ref, cb_ref, wg_ref, ba_ref, bx_ref,
                    lam_ref, h0_ref, h_ref, uh_ref, carry_out_ref, hext_scr, rec0_scr, rec1_scr, a_scr, b_scr,
                    h_scr, carry_ref, *, tm, nt):
    tile = nt - 1 - pl.program_id(1)
    shift, scale = mod_ref[0, 0:1, :], mod_ref[0, 1:2, :]
    gm = gpre_ref[...] * (1.0 + scale)
    half_cw, half_cb = 0.5 * cw_ref[...], 0.5 * cb_ref[...]

    rec_scrs = (rec0_scr, rec1_scr)
    assert len(rec_scrs) * HEADS_PER_TILE == RG_HEADS
    n_ext = tm + 2 * HALO
    rows_a = -(-(HALO + tm // 2) // BF16_ROWS) * BF16_ROWS

    def rec_chunk(p, lo, hi):
        r = jnp.dot(hext_scr[lo:hi], wr_ref[:, p * MXU_N:(p + 1) * MXU_N], preferred_element_type=F32)
        for q in range(HEADS_PER_TILE):
            rec_scrs[p][q, lo:hi] = r[:, q * RG_BLOCK:(q + 1) * RG_BLOCK]

    hp = jnp.where(tile > 0, _norm_mod(xp_ref[0], gm, shift), 0.0)
    hext_scr[:rows_a] = jnp.concatenate([hp, _norm_mod(x_ref[0, :rows_a - HALO], gm, shift)],
                                        axis=0).astype(BF16)
    rec_chunk(0, 0, rows_a)
    hn = jnp.where(tile < nt - 1, _norm_mod(xn_ref[0], gm, shift), 0.0)
    hext_scr[rows_a:] = jnp.concatenate([_norm_mod(x_ref[0, rows_a - HALO:], gm, shift), hn],
                                        axis=0).astype(BF16)
    rec_chunk(0, rows_a, n_ext)

    def after_gate_matmul(hd):
        p, q = divmod(hd, HEADS_PER_TILE)
        if q == HEADS_PER_TILE - 1 and p + 1 < len(rec_scrs):
            rec_chunk(p + 1, 0, n_ext)

    def uh_of_head(hd):
        p, q = divmod(hd, HEADS_PER_TILE)
        ln = slice(hd * RG_BLOCK, (hd + 1) * RG_BLOCK)
        uh = half_cb[:, ln] + sum(half_cw[k:k + 1, ln] * _shifted(rec_scrs[p], q, k - RG_CONV // 2, tm)
                                  for k in range(RG_CONV))
        uh_ref[0, :, ln] = uh
        return uh

    _rg_gates_scan(uh_of_head, after_gate_matmul, wg_ref, ba_ref, bx_ref, lam_ref, h0_ref, carry_out_ref,
                   a_scr, b_scr, h_scr, carry_ref, tm=tm, nt=nt, reverse=True)
    for hd in range(RG_HEADS):
        h_ref[0, :, hd * RG_BLOCK:(hd + 1) * RG_BLOCK] = _head_h(h_scr, hd, tm)


def _rg_final_kernel(x_ref, mod_ref, gpre_ref, uh_ref, wg_ref, ba_ref, bx_ref, lam_ref, h0_ref, hb_ref,
                     wgate_ref, wout_ref, gpost_ref, o_ref, carry_out_ref, a_scr, b_scr, h_scr, carry_ref,
                     hm_scr, gz_scr, gy_scr, *, tm, nt):
    shift, scale, gate = mod_ref[0, 0:1, :], mod_ref[0, 1:2, :], mod_ref[0, 2:3, :]
    hm_scr[...] = _norm_mod(x_ref[0], gpre_ref[...] * (1.0 + scale), shift).astype(BF16)

    def after_gate_matmul(hd):
        pass

    def uh_of_head(hd):
        if hd % 2 == 0:
            ln2 = slice(hd * RG_BLOCK, (hd + 2) * RG_BLOCK)
            gz_scr[:, ln2] = jnp.dot(hm_scr[...], wgate_ref[:, ln2], preferred_element_type=F32)
        return uh_ref[0, :, hd * RG_BLOCK:(hd + 1) * RG_BLOCK]

    _rg_gates_scan(uh_of_head, after_gate_matmul, wg_ref, ba_ref, bx_ref, lam_ref, h0_ref, carry_out_ref,
                   a_scr, b_scr, h_scr, carry_ref, tm=tm, nt=nt, reverse=False)
    out = None
    for part in range(OUT_PARTS):
        kk = slice(part * (D_MODEL // OUT_PARTS), (part + 1) * (D_MODEL // OUT_PARTS))
        for hd in range(part * RG_HEADS // OUT_PARTS, (part + 1) * RG_HEADS // OUT_PARTS):
            ln = slice(hd * RG_BLOCK, (hd + 1) * RG_BLOCK)
            gy_scr[:, ln] = (_gelu(gz_scr[:, ln]) * (_head_h(h_scr, hd, tm) + hb_ref[0, :, ln])).astype(BF16)
        partial = jnp.dot(gy_scr[:, kk], wout_ref[kk, :], preferred_element_type=F32)
        out = partial if out is None else out + partial
    o_ref[0] = x_ref[0] + _norm_mod(out, gate * gpost_ref[...], None)


def _scan_scratch(tm):
    rows = SCAN_SEGS * _seg_pitch(tm // SCAN_SEGS)
    return [pltpu.VMEM((RG_HEADS, rows, RG_BLOCK), F32)] * 3 + [pltpu.VMEM((RG_HEADS, 1, RG_BLOCK), F32)]


def _rg_reverse(x, mod, g_pre, w_rec, conv_w, conv_b, w_gates, b_a, b_x, lam, h0, *, tm):
    b, t, d = x.shape
    nt = t // tm
    main, prev, nxt = _tile_specs(tm, nt, True)
    carry_spec = pl.BlockSpec((1, SUBLANES, d), lambda bb, j: (bb, 0, 0))
    return pl.pallas_call(
        functools.partial(_rg_scan_kernel, tm=tm, nt=nt),
        grid=(b, nt),
        in_specs=[main, prev, nxt, _mod_spec(mod.shape[0]), _const_spec((1, d)), _const_spec(w_rec.shape),
                  _const_spec(conv_w.shape), _const_spec((1, d)), _const_spec(w_gates.shape),
                  _const_spec((1, d)), _const_spec((1, d)), _const_spec((1, d)), carry_spec],
        out_specs=[main, main, carry_spec],
        out_shape=[jax.ShapeDtypeStruct(x.shape, F32), jax.ShapeDtypeStruct(x.shape, F32),
                   jax.ShapeDtypeStruct((b, SUBLANES, d), F32)],
        scratch_shapes=[pltpu.VMEM((tm + 2 * HALO, d), BF16),
                        pltpu.VMEM((HEADS_PER_TILE, tm + 2 * HALO, RG_BLOCK), F32),
                        pltpu.VMEM((HEADS_PER_TILE, tm + 2 * HALO, RG_BLOCK), F32)] + _scan_scratch(tm),
        compiler_params=pltpu.CompilerParams(dimension_semantics=("arbitrary", "arbitrary"),
                                             vmem_limit_bytes=VMEM_LIMIT),
        name="rg_scan",
    )(x, x, x, mod, g_pre, w_rec, conv_w, conv_b, w_gates, b_a, b_x, lam, h0)


def _rg_forward(x, mod, g_pre, uh, w_gates, b_a, b_x, lam, h0, h_rev, w_gate, w_out, g_post, *, tm):
    b, t, d = x.shape
    nt = t // tm
    main = pl.BlockSpec((1, tm, d), lambda bb, j: (bb, j, 0))
    carry_spec = pl.BlockSpec((1, SUBLANES, d), lambda bb, j: (bb, 0, 0))
    return pl.pallas_call(
        functools.partial(_rg_final_kernel, tm=tm, nt=nt),
        grid=(b, nt),
        in_specs=[main, _mod_spec(mod.shape[0]), _const_spec((1, d)), main, _const_spec(w_gates.shape),
                  _const_spec((1, d)), _const_spec((1, d)), _const_spec((1, d)), carry_spec, main,
                  _const_spec(w_gate.shape), _const_spec(w_out.shape), _const_spec((1, d))],
        out_specs=[main, carry_spec],
        out_shape=[jax.ShapeDtypeStruct(x.shape, F32), jax.ShapeDtypeStruct((b, SUBLANES, d), F32)],
        scratch_shapes=_scan_scratch(tm) + [pltpu.VMEM((tm, d), BF16), pltpu.VMEM((tm, d), F32),
                                            pltpu.VMEM((tm, d), BF16)],
        compiler_params=pltpu.CompilerParams(dimension_semantics=("arbitrary", "arbitrary"),
                                             vmem_limit_bytes=VMEM_LIMIT),
        name="rg_final",
    )(x, mod, g_pre, uh, w_gates, b_a, b_x, lam, h0, h_rev, w_gate, w_out, g_post)


def _pool_kernel(*refs, tm, nt, width, halo_rows):
    if halo_rows:
        (x_ref, xp_ref, xn_ref, mod_ref, gpre_ref, gpost_ref, band_ref, invcc_ref, rowinv_ref, pw_ref, pb_ref,
         ps_ref, o_ref, hext_scr, rs_scr) = refs
    else:
        (x_ref, mod_ref, gpre_ref, gpost_ref, band_ref, invcc_ref, rowinv_ref, pw_ref, pb_ref, ps_ref,
         o_ref, hext_scr, rs_scr) = refs
    j = pl.program_id(1)
    rt = tm // width
    shift, scale, gate = mod_ref[0, 0:1, :], mod_ref[0, 1:2, :], mod_ref[0, 2:3, :]
    gm = gpre_ref[...] * (1.0 + scale)

    def prep(v):
        return _norm_mod(v, gm, shift)

    base = halo_rows * width
    hext_scr[base:base + tm] = prep(x_ref[0])
    def halo_row(src_ref, i, dst_row, reach, valid):
        groups = [g for g, w in enumerate(POOL_WINDOWS) if w // 2 >= reach]
        if groups:
            lanes = slice(groups[0] * POOL_GROUP, D_MODEL)
            v = src_ref[0, i * width:(i + 1) * width, :]
            r = lax.rsqrt(jnp.mean(v * v, axis=-1, keepdims=True) + EPS)
            y = (v[:, lanes] * r) * gm[:, lanes] + shift[:, lanes]
            hext_scr[dst_row * width:(dst_row + 1) * width, lanes] = jnp.where(valid, y, 0.0)

    for i in range(halo_rows):
        halo_row(xp_ref, i, i, halo_rows - i, j > 0)
        halo_row(xn_ref, i, halo_rows + rt + i, i + 2, j < nt - 1)

    def grid_row(r, ln):
        return hext_scr[(r + halo_rows) * width:(r + halo_rows + 1) * width, ln]

    for g, w in enumerate(POOL_WINDOWS):
        ln = slice(g * POOL_GROUP, (g + 1) * POOL_GROUP)
        if halo_rows:
            win = grid_row(-(w // 2), ln)
            for dr in range(-(w // 2) + 1, w // 2):
                win = win + grid_row(dr, ln)
        else:
            win = grid_row(0, ln)
        for r in range(rt):
            rs_scr[r * width:(r + 1) * width, ln] = win * rowinv_ref[pl.ds(j * rt + r, 1), ln]
            if r + 1 < rt:
                win = win + grid_row(r + w // 2, ln) - grid_row(r - w // 2, ln)

    blocks = [(ch, g) for ch in range(tm // POOL_CHUNK) for g in range(len(POOL_WINDOWS))]

    def window_sums(ch, g):
        ln = slice(g * POOL_GROUP, (g + 1) * POOL_GROUP)
        rs = rs_scr[ch * POOL_CHUNK:(ch + 1) * POOL_CHUNK, ln]
        hi = rs.astype(BF16)
        lo = (rs - hi.astype(F32)).astype(BF16)
        return (jnp.dot(band_ref[g], hi, preferred_element_type=F32)
                + jnp.dot(band_ref[g], lo, preferred_element_type=F32))

    def group_map(ch, g, sums):
        ln = slice(g * POOL_GROUP, (g + 1) * POOL_GROUP)
        hg = hext_scr[base + ch * POOL_CHUNK:base + (ch + 1) * POOL_CHUNK, ln]
        yg = jnp.dot((sums * invcc_ref[:, ln] - hg).astype(BF16), pw_ref[g], preferred_element_type=F32)
        rs_scr[ch * POOL_CHUNK:(ch + 1) * POOL_CHUNK, ln] = (yg + pb_ref[:, ln]) * ps_ref[:, ln]

    sums = window_sums(*blocks[0])
    for k, blk in enumerate(blocks):
        nxt = window_sums(*blocks[k + 1]) if k + 1 < len(blocks) else None
        group_map(*blk, sums)
        sums = nxt

    o_ref[0] = x_ref[0] + _norm_mod(rs_scr[...], gate * gpost_ref[...], None)


def _pool_consts(n_rows, width):
    idx = np.arange(POOL_CHUNK)
    row, col = idx // width, idx % width
    band = np.zeros((len(POOL_WINDOWS), POOL_CHUNK, POOL_CHUNK), np.float32)
    invcc = np.zeros((POOL_CHUNK, D_MODEL), np.float32)
    rowinv = np.zeros((n_rows, D_MODEL), np.float32)
    r = np.arange(n_rows)
    for g, w in enumerate(POOL_WINDOWS):
        same = row[:, None] == row[None, :]
        d = col[None, :] - col[:, None]
        band[g] = (same & (d >= -(w // 2)) & (d < w // 2)).astype(np.float32)
        cc = np.minimum(col + w // 2, width) - np.maximum(col - w // 2, 0)
        invcc[:, g * POOL_GROUP:(g + 1) * POOL_GROUP] = (1.0 / cc)[:, None]
        cr = np.minimum(r + w // 2, n_rows) - np.maximum(r - w // 2, 0)
        rowinv[:, g * POOL_GROUP:(g + 1) * POOL_GROUP] = (1.0 / cr)[:, None]
    return jnp.asarray(band, BF16), jnp.asarray(invcc), jnp.asarray(rowinv)


def _pool(x, mod, g_pre, g_post, pw, pb, ps, *, width, tm):
    b, t, d = x.shape
    n_rows = t // width
    nt = t // tm
    halo_rows = max(POOL_WINDOWS) // 2 if nt > 1 else 0
    assert nt > 1 or n_rows == 1
    band, invcc, rowinv = _pool_consts(n_rows, width)
    main = pl.BlockSpec((1, tm, d), lambda bb, j: (bb, j, 0))
    in_specs, args = [main], [x]
    if halo_rows:
        hb = halo_rows * width
        per = tm // hb
        last = t // hb - 1
        in_specs += [pl.BlockSpec((1, hb, d), lambda bb, j: (bb, jnp.maximum(j * per - 1, 0), 0)),
                     pl.BlockSpec((1, hb, d), lambda bb, j: (bb, jnp.minimum((j + 1) * per, last), 0))]
        args += [x, x]
    in_specs += [_mod_spec(mod.shape[0]), _const_spec((1, d)), _const_spec((1, d)), _const_spec(band.shape),
                 _const_spec(invcc.shape), _const_spec(rowinv.shape), _const_spec(pw.shape),
                 _const_spec((1, d)), _const_spec((1, d))]
    args += [mod, g_pre, g_post, band, invcc, rowinv, pw, pb, ps]
    kern = functools.partial(_pool_kernel, tm=tm, nt=nt, width=width, halo_rows=halo_rows)
    return pl.pallas_call(
        kern,
        grid=(b, nt),
        in_specs=in_specs,
        out_specs=main,
        out_shape=jax.ShapeDtypeStruct(x.shape, F32),
        scratch_shapes=[pltpu.VMEM((tm + 2 * halo_rows * width, d), F32), pltpu.VMEM((tm, d), F32)],
        compiler_params=pltpu.CompilerParams(dimension_semantics=("parallel", "parallel"),
                                             vmem_limit_bytes=VMEM_LIMIT),
        name="pool_mix",
    )(*args)


def kernel(x, c, ctx, c_ctx, ada_w, ada_b, norm_pre_mix, norm_post_mix, norm_pre_ffn, norm_post_ffn, rg_w_in,
           rg_conv_w, rg_conv_b, rg_gate_a_w, rg_gate_a_b, rg_gate_x_w, rg_gate_x_b, rg_lambda, rg_w_out, pool_w,
           pool_b, pool_scale, ffn_w_up, ffn_conv_w, ffn_conv_b, ffn_w_down):
    bsz, seq, d = x.shape
    ctx_len = ctx.shape[1]
    depth = ada_w.shape[0]
    assert d == D_MODEL and seq % GRID_W == 0

    ct = jnp.zeros((d, SUBLANES), F32).at[:, :bsz].set(c.T).at[:, bsz].set(c_ctx)
    mods = _ada(ct, ada_w, ada_b, bsz + 1)
    row = lambda a, i: a[i].reshape(1, -1)
    tm_lat = 512
    zeros_state = jnp.zeros((bsz, SUBLANES, d), F32)

    for i in range(depth):
        last = i == depth - 1
        mod_l = mods[i, :bsz].reshape(bsz, N_MOD, d)
        mod_c = mods[i, bsz].reshape(1, N_MOD, d)
        g_pre, g_post = row(norm_pre_mix, i), row(norm_post_mix, i)
        jm = i // 2
        if i % 2 == 0:
            w_gate = rg_w_in[jm, :, :d].astype(BF16)
            w_rec = rg_w_in[jm, :, d:].astype(BF16)
            w_out = rg_w_out[jm].astype(BF16)
            conv_w, conv_b = rg_conv_w[jm], row(rg_conv_b, jm)
            dirs = []
            for dd in range(2):
                wg = jnp.concatenate([rg_gate_a_w[jm, dd], rg_gate_x_w[jm, dd]], axis=-1)
                zg = jnp.zeros_like(wg[0::2])
                wg = jnp.concatenate([jnp.concatenate([wg[0::2], zg], axis=-1),
                                      jnp.concatenate([zg, wg[1::2]], axis=-1)], axis=1).astype(BF16)
                dirs.append((wg, rg_gate_a_b[jm, dd].reshape(1, d), rg_gate_x_b[jm, dd].reshape(1, d),
                             rg_lambda[jm, dd].reshape(1, d)))
            hb_c, uh_c, end_b = _rg_reverse(ctx, mod_c, g_pre, w_rec, conv_w, conv_b, *dirs[1], zeros_state,
                                            tm=ctx_len)
            ctx_new, end_f = _rg_forward(ctx, mod_c, g_pre, uh_c, *dirs[0], zeros_state, hb_c, w_gate, w_out,
                                         g_post, tm=ctx_len)
            hb_l, uh_l, _ = _rg_reverse(x, mod_l, g_pre, w_rec, conv_w, conv_b, *dirs[1], end_b, tm=1024)
            x, _ = _rg_forward(x, mod_l, g_pre, uh_l, *dirs[0], end_f, hb_l, w_gate, w_out, g_post, tm=tm_lat)
            if not last:
                ctx = ctx_new
        else:
            pw = pool_w[jm].astype(BF16)
            pb, ps = row(pool_b, jm), row(pool_scale, jm)
            x_new = _pool(x, mod_l, g_pre, g_post, pw, pb, ps, width=GRID_W, tm=1024)
            if not last:
                ctx = _pool(ctx, mod_c, g_pre, g_post, pw, pb, ps, width=ctx_len, tm=ctx_len)
            x = x_new

        g_pre, g_post = row(norm_pre_ffn, i), row(norm_post_ffn, i)
        w_up, w_dn = _pack_up_weights(ffn_w_up, i, FFN_CHUNK), ffn_w_down[i].astype(BF16)
        cw, cb = ffn_conv_w[i], row(ffn_conv_b, i)
        x = _ffn(x, mod_l, g_pre, g_post, w_up, cw, cb, w_dn, tm=1024)
        if not last:
            ctx = _ffn(ctx, mod_c, g_pre, g_post, w_up, cw, cb, w_dn, tm=ctx_len)
    return x
```
